```python
import math
import jax, jax.numpy as jnp
from jax import lax
import numpy as np

D_MODEL = 2048
BATCH = 4
SEQ = 2048
DEPTH = 2
DEC_BATCH = 128
DEC_SEQ = 8
PAST_LEN = 16384
PAGE_SIZE = 128

D_MIX = D_MODEL
CONV_WIDTH = 4
SSD_WIDTH = D_MIX // 2
SSD_HEAD_DIM = 64
SSD_HEADS = SSD_WIDTH // SSD_HEAD_DIM
SSD_GROUPS = 2
SSD_STATE = 128
SSD_CHUNK = 64
SSD_CONV_DIM = SSD_WIDTH + 2 * SSD_GROUPS * SSD_STATE
GDN_WIDTH = D_MIX - SSD_WIDTH
GDN_HEAD_K = 128
GDN_HEAD_V = 128
GDN_V_HEADS = GDN_WIDTH // GDN_HEAD_V
GDN_K_HEADS = GDN_V_HEADS // 2
GDN_QK_DIM = GDN_K_HEADS * GDN_HEAD_K
GDN_CONV_DIM = 2 * GDN_QK_DIM + GDN_WIDTH
GDN_CHUNK = 64
IN_PROJ_DIM = SSD_WIDTH + SSD_CONV_DIM + SSD_HEADS + GDN_CONV_DIM + GDN_WIDTH + 2 * GDN_V_HEADS
D_FF = 5632
N_EXPERTS = 8
TOP_K = 2
D_FF_EXPERT = 7168
N_DENSE = (DEPTH + 1) // 2
N_MOE = DEPTH // 2
N_MOD = 6
EPS = 1e-6
DT_MIN = 1e-3
DT_MAX = 1e-1

kernel_name = "hymba_ssd_gdn_adaln_moe_step"


def _rms(x, w):
    xf = x.astype(jnp.float32)
    xf = xf * lax.rsqrt(jnp.mean(xf * xf, axis=-1, keepdims=True) + EPS)
    return (xf * w.astype(jnp.float32)).astype(x.dtype)


def _l2norm(x):
    return x * lax.rsqrt(jnp.sum(x * x, axis=-1, keepdims=True) + EPS)


def _causal_dwconv(u, buf, w, b):
    full = jnp.concatenate([buf.astype(u.dtype), u], axis=1)
    y = lax.conv_general_dilated(full, w[:, None, :].astype(u.dtype), (1,), "VALID",
                                 dimension_numbers=("NWC", "WIO", "NWC"),
                                 feature_group_count=u.shape[-1])
    if b is not None:
        y = y + b
    return jax.nn.silu(y), full[:, full.shape[1] - (CONV_WIDTH - 1):]


def _segsum_mask(a_cum):
    q = a_cum.shape[-1]
    tri = jnp.tril(jnp.ones((q, q), dtype=bool))
    return jnp.where(tri, a_cum[..., :, None] - a_cum[..., None, :], -jnp.inf)


def _ssd_chunked(x, dt, A, Bm, Cm, s0):
    b, L, h, p = x.shape
    Q = math.gcd(L, SSD_CHUNK)
    nc = L // Q
    rep = h // Bm.shape[2]
    Bh = jnp.repeat(Bm, rep, axis=2)
    Ch = jnp.repeat(Cm, rep, axis=2)
    r = lambda t: t.reshape((b, nc, Q) + t.shape[2:])
    xc, Bc, Cc = r(x * dt[..., None]), r(Bh), r(Ch)
    a_cum = jnp.moveaxis(jnp.cumsum(r(dt * A), axis=2), 3, 2)
    Lmat = jnp.exp(_segsum_mask(a_cum))
    y_diag = jnp.einsum("bclhn,bcshn,bchls,bcshp->bclhp", Cc, Bc, Lmat, xc)
    decay_to_end = jnp.exp(a_cum[..., -1:] - a_cum)
    chunk_states = jnp.einsum("bcshn,bchs,bcshp->bchpn", Bc, decay_to_end, xc)
    chunk_decay = jnp.exp(a_cum[..., -1])

    def step(s, inp):
        st, dec = inp
        return s * dec[..., None, None] + st, s

    s_final, s_before = lax.scan(step, s0, (jnp.moveaxis(chunk_states, 1, 0), jnp.moveaxis(chunk_decay, 1, 0)))
    s_before = jnp.moveaxis(s_before, 0, 1)
    y_off = jnp.einsum("bclhn,bchpn,bchl->bclhp", Cc, s_before, jnp.exp(a_cum))
    return (y_diag + y_off).reshape(b, L, h, p), s_final


def _gated_delta_chunked(q, k, v, g, beta, s0):
    b, L, h, dk = q.shape
    dv = v.shape[-1]
    Q = math.gcd(L, GDN_CHUNK)
    nc = L // Q
    r = lambda t: jnp.moveaxis(t.reshape((b, nc, Q) + t.shape[2:]), 3, 2)
    qc, kc, vc, gc, bc = r(q), r(k), r(v), r(g), r(beta)
    g_cum = jnp.cumsum(gc, axis=-1)
    decay = jnp.exp(_segsum_mask(g_cum))
    kb = kc * bc[..., None]
    strict = jnp.tril(jnp.ones((Q, Q), dtype=bool), -1)
    m = jnp.where(strict, jnp.einsum("bchid,bchjd->bchij", kb, kc) * decay, 0.0)
    rhs = jnp.concatenate([vc * bc[..., None], kb * jnp.exp(g_cum)[..., None]], axis=-1)
    sol = lax.linalg.triangular_solve(m, rhs, left_side=True, lower=True, unit_diagonal=True)
    u, w = sol[..., :dv], sol[..., dv:]
    attn_intra = jnp.einsum("bchid,bchjd->bchij", qc, kc) * decay
    q_dec = qc * jnp.exp(g_cum)[..., None]
    k_dec = kc * jnp.exp(g_cum[..., -1:] - g_cum)[..., None]
    chunk_dec = jnp.exp(g_cum[..., -1])

    def step(S, inp):
        u_i, w_i, qd_i, kd_i, a_i, dec_i = inp
        v_new = u_i - jnp.einsum("bhid,bhde->bhie", w_i, S)
        o = jnp.einsum("bhid,bhde->bhie", qd_i, S) + jnp.einsum("bhij,bhje->bhie", a_i, v_new)
        S = S * dec_i[..., None, None] + jnp.einsum("bhjd,bhje->bhde", kd_i, v_new)
        return S, o

    xs = tuple(jnp.moveaxis(t, 1, 0) for t in (u, w, q_dec, k_dec, attn_intra, chunk_dec))
    s_final, o = lax.scan(step, s0, xs)
    o = jnp.moveaxis(jnp.moveaxis(o, 0, 1), 2, 3).reshape(b, L, h, dv)
    return o, s_final


def _mixer(h, ssd_conv_buf, ssm_state, gdn_conv_buf, gdn_state, lp):
    bsz, L, _ = h.shape
    f32 = jnp.float32
    cuts = np.cumsum([SSD_WIDTH, SSD_CONV_DIM, SSD_HEADS, GDN_CONV_DIM, GDN_WIDTH, GDN_V_HEADS]).tolist()
    z_ssd, xbc, dt_raw, qkv, z_gdn, a_raw, b_raw = jnp.split(h @ lp["w_in"], cuts, axis=-1)
    xbc, ssd_conv_new = _causal_dwconv(xbc, ssd_conv_buf, lp["w_conv_ssd"], lp["b_conv_ssd"])
    xs, Bm, Cm = jnp.split(xbc, [SSD_WIDTH, SSD_WIDTH + SSD_GROUPS * SSD_STATE], axis=-1)
    xs = xs.reshape(bsz, L, SSD_HEADS, SSD_HEAD_DIM).astype(f32)
    Bm = Bm.reshape(bsz, L, SSD_GROUPS, SSD_STATE).astype(f32)
    Cm = Cm.reshape(bsz, L, SSD_GROUPS, SSD_STATE).astype(f32)
    dt = jax.nn.softplus(dt_raw.astype(f32) + lp["ssd_dt_bias"].astype(f32))
    A = -jnp.exp(lp["ssd_a_log"].astype(f32))
    y, ssm_new = _ssd_chunked(xs, dt, A, Bm, Cm, ssm_state.astype(f32))
    y = y + lp["ssd_d"].astype(f32)[:, None] * xs
    gshape = (bsz, L, SSD_GROUPS, SSD_WIDTH // SSD_GROUPS)
    y = y.reshape(gshape) * jax.nn.silu(z_ssd.astype(f32)).reshape(gshape)
    y = y * lax.rsqrt(jnp.mean(y * y, axis=-1, keepdims=True) + EPS)
    y_ssd = y.reshape(bsz, L, SSD_WIDTH) * lp["ssd_norm"].astype(f32)
    qkv, gdn_conv_new = _causal_dwconv(qkv, gdn_conv_buf, lp["w_conv_gdn"], None)
    q, k, v = jnp.split(qkv, [GDN_QK_DIM, 2 * GDN_QK_DIM], axis=-1)
    rep = GDN_V_HEADS // GDN_K_HEADS
    q = jnp.repeat(_l2norm(q.reshape(bsz, L, GDN_K_HEADS, GDN_HEAD_K).astype(f32)), rep, axis=2) * (GDN_HEAD_K ** -0.5)
    k = jnp.repeat(_l2norm(k.reshape(bsz, L, GDN_K_HEADS, GDN_HEAD_K).astype(f32)), rep, axis=2)
    v = v.reshape(bsz, L, GDN_V_HEADS, GDN_HEAD_V).astype(f32)
    beta = jax.nn.sigmoid(b_raw.astype(f32))
    g = -jnp.exp(lp["gdn_a_log"].astype(f32)) * jax.nn.softplus(a_raw.astype(f32) + lp["gdn_dt_bias"].astype(f32))
    o, gdn_new = _gated_delta_chunked(q, k, v, g, beta, gdn_state.astype(f32))
    o = o * lax.rsqrt(jnp.mean(o * o, axis=-1, keepdims=True) + EPS) * lp["gdn_norm"].astype(f32)
    o = o * jax.nn.silu(z_gdn.reshape(bsz, L, GDN_V_HEADS, GDN_HEAD_V).astype(f32))
    y_gdn = o.reshape(bsz, L, GDN_WIDTH)
    mix = jnp.concatenate([y_ssd, y_gdn], axis=-1).astype(h.dtype) @ lp["w_out"]
    return mix, ssd_conv_new, ssm_new.astype(h.dtype), gdn_conv_new, gdn_new.astype(h.dtype)


def _swiglu(h, w_gate, w_up, w_down):
    return (jax.nn.silu(h @ w_gate) * (h @ w_up)) @ w_down


def _moe(h, w_router, w_gate, w_up, w_down):
    logits = (h @ w_router).astype(jnp.float32)
    top_logit, top_idx = lax.top_k(logits, TOP_K)
    top_w = jax.nn.softmax(top_logit, axis=-1)
    combine = jnp.einsum("...k,...ke->...e", top_w, jax.nn.one_hot(top_idx, N_EXPERTS, dtype=jnp.float32)).astype(h.dtype)
    out = jnp.zeros_like(h)
    for e in range(N_EXPERTS):
        out = out + combine[..., e:e + 1] * _swiglu(h, w_gate[e], w_up[e], w_down[e])
    return out


def _run_trunk(x, c, ssd_conv, ssm, gdn_conv, gdn, wts):
    new_sc, new_ssm, new_gc, new_gdn = [], [], [], []
    for l in range(DEPTH):
        mod = jax.nn.silu(c) @ wts["w_ada"][l] + wts["b_ada"][l]
        sh1, sc1, g1, sh2, sc2, g2 = [m[:, None, :] for m in jnp.split(mod, N_MOD, axis=-1)]
        h = _rms(x, wts["w_norm_mix"][l]) * (1 + sc1) + sh1
        lp = {"w_in": wts["w_in"][l], "w_conv_ssd": wts["w_conv_ssd"][l], "b_conv_ssd": wts["b_conv_ssd"][l],
              "ssd_dt_bias": wts["ssd_dt_bias"][l], "ssd_a_log": wts["ssd_a_log"][l], "ssd_d": wts["ssd_d"][l],
              "ssd_norm": wts["ssd_norm"][l], "w_conv_gdn": wts["w_conv_gdn"][l], "gdn_dt_bias": wts["gdn_dt_bias"][l],
              "gdn_a_log": wts["gdn_a_log"][l], "gdn_norm": wts["gdn_norm"][l], "w_out": wts["w_out"][l]}
        mix, sc_n, ssm_n, gc_n, gdn_n = _mixer(h, ssd_conv[l], ssm[l], gdn_conv[l], gdn[l], lp)
        new_sc.append(sc_n)
        new_ssm.append(ssm_n)
        new_gc.append(gc_n)
        new_gdn.append(gdn_n)
        x = x + g1 * mix
        h = _rms(x, wts["w_norm_ffn"][l]) * (1 + sc2) + sh2
        i = l // 2
        if l % 2 == 0:
            f = _swiglu(h, wts["w_ffn_gate"][i], wts["w_ffn_up"][i], wts["w_ffn_down"][i])
        else:
            f = _moe(h, wts["w_router"][i], wts["w_exp_gate"][i], wts["w_exp_up"][i], wts["w_exp_down"][i])
        x = x + g2 * f
    y = _rms(x, wts["w_norm_final"])
    return y, jnp.stack(new_sc), jnp.stack(new_ssm), jnp.stack(new_gc), jnp.stack(new_gdn)


def setup_inputs(seed: int = 0) -> dict:
    key = jax.random.key(seed)
    ks = jax.random.split(key, 40)
    counter = iter(range(40))
    nk = lambda: ks[next(counter)]
    f32 = jnp.float32
    nrm = lambda shape, s: jax.random.normal(nk(), shape, f32) * s
    gain = lambda shape: 1.0 + nrm(shape, 0.02)

    def dt_bias(shape):
        u = jax.random.uniform(nk(), shape, f32)
        dt = jnp.exp(u * (math.log(DT_MAX) - math.log(DT_MIN)) + math.log(DT_MIN))
        return dt + jnp.log(-jnp.expm1(-dt))

    a_log = lambda shape: jnp.log(jax.random.uniform(nk(), shape, f32, 1.0, 16.0))
    D = D_MODEL
    return {
        "x_prompt": nrm((BATCH, SEQ, D), 1.0),
        "x_sample": nrm((DEC_BATCH, DEC_SEQ, D), 1.0),
        "c_prompt": nrm((BATCH, D), 1.0),
        "c_sample": nrm((DEC_BATCH, D), 1.0),
        "state_ssd_conv": nrm((DEPTH, DEC_BATCH, CONV_WIDTH - 1, SSD_CONV_DIM), 1.0),
        "state_ssm": nrm((DEPTH, DEC_BATCH, SSD_HEADS, SSD_HEAD_DIM, SSD_STATE), 0.1),
        "state_gdn_conv": nrm((DEPTH, DEC_BATCH, CONV_WIDTH - 1, GDN_CONV_DIM), 1.0),
        "state_gdn": nrm((DEPTH, DEC_BATCH, GDN_V_HEADS, GDN_HEAD_K, GDN_HEAD_V), 0.1),
        "w_ada": nrm((DEPTH, D, N_MOD * D), 0.5 * D ** -0.5),
        "b_ada": nrm((DEPTH, N_MOD * D), 0.02),
        "w_norm_mix": gain((DEPTH, D)),
        "w_norm_ffn": gain((DEPTH, D)),
        "w_in": nrm((DEPTH, D, IN_PROJ_DIM), D ** -0.5),
        "w_conv_ssd": nrm((DEPTH, CONV_WIDTH, SSD_CONV_DIM), CONV_WIDTH ** -0.5),
        "b_conv_ssd": nrm((DEPTH, SSD_CONV_DIM), 0.02),
        "ssd_dt_bias": dt_bias((DEPTH, SSD_HEADS)),
        "ssd_a_log": a_log((DEPTH, SSD_HEADS)),
        "ssd_d": 1.0 + nrm((DEPTH, SSD_HEADS), 0.1),
        "ssd_norm": gain((DEPTH, SSD_WIDTH)),
        "w_conv_gdn": nrm((DEPTH, CONV_WIDTH, GDN_CONV_DIM), CONV_WIDTH ** -0.5),
        "gdn_dt_bias": dt_bias((DEPTH, GDN_V_HEADS)),
        "gdn_a_log": a_log((DEPTH, GDN_V_HEADS)),
        "gdn_norm": gain((DEPTH, GDN_HEAD_V)),
        "w_out": nrm((DEPTH, D_MIX, D), D_MIX ** -0.5),
        "w_ffn_gate": nrm((N_DENSE, D, D_FF), D ** -0.5),
        "w_ffn_up": nrm((N_DENSE, D, D_FF), D ** -0.5),
        "w_ffn_down": nrm((N_DENSE, D_FF, D), D_FF ** -0.5),
        "w_router": nrm((N_MOE, D, N_EXPERTS), D ** -0.5),
        "w_exp_gate": nrm((N_MOE, N_EXPERTS, D, D_FF_EXPERT), D ** -0.5),
        "w_exp_up": nrm((N_MOE, N_EXPERTS, D, D_FF_EXPERT), D ** -0.5),
        "w_exp_down": nrm((N_MOE, N_EXPERTS, D_FF_EXPERT, D), D_FF_EXPERT ** -0.5),
        "w_norm_final": gain((D,)),
    }


def reference(x_prompt, x_sample, c_prompt, c_sample, state_ssd_conv, state_ssm, state_gdn_conv, state_gdn,
              w_ada, b_ada, w_norm_mix, w_norm_ffn, w_in, w_conv_ssd, b_conv_ssd, ssd_dt_bias, ssd_a_log, ssd_d,
              ssd_norm, w_conv_gdn, gdn_dt_bias, gdn_a_log, gdn_norm, w_out, w_ffn_gate, w_ffn_up, w_ffn_down,
              w_router, w_exp_gate, w_exp_up, w_exp_down, w_norm_final):
    wts = {"w_ada": w_ada, "b_ada": b_ada, "w_norm_mix": w_norm_mix, "w_norm_ffn": w_norm_ffn, "w_in": w_in,
           "w_conv_ssd": w_conv_ssd, "b_conv_ssd": b_conv_ssd, "ssd_dt_bias": ssd_dt_bias, "ssd_a_log": ssd_a_log,
           "ssd_d": ssd_d, "ssd_norm": ssd_norm, "w_conv_gdn": w_conv_gdn, "gdn_dt_bias": gdn_dt_bias,
           "gdn_a_log": gdn_a_log, "gdn_norm": gdn_norm, "w_out": w_out, "w_ffn_gate": w_ffn_gate,
           "w_ffn_up": w_ffn_up, "w_ffn_down": w_ffn_down, "w_router": w_router, "w_exp_gate": w_exp_gate,
           "w_exp_up": w_exp_up, "w_exp_down": w_exp_down, "w_norm_final": w_norm_final}
    bp = x_prompt.shape[0]
    dtp = x_prompt.dtype
    z_sc = jnp.zeros((DEPTH, bp, CONV_WIDTH - 1, SSD_CONV_DIM), dtp)
    z_ssm = jnp.zeros((DEPTH, bp, SSD_HEADS, SSD_HEAD_DIM, SSD_STATE), dtp)
    z_gc = jnp.zeros((DEPTH, bp, CONV_WIDTH - 1, GDN_CONV_DIM), dtp)
    z_gdn = jnp.zeros((DEPTH, bp, GDN_V_HEADS, GDN_HEAD_K, GDN_HEAD_V), dtp)
    y_prompt, p_sc, p_ssm, p_gc, p_gdn = _run_trunk(x_prompt, c_prompt, z_sc, z_ssm, z_gc, z_gdn, wts)
    y_sample, s_sc, s_ssm, s_gc, s_gdn = _run_trunk(x_sample, c_sample, state_ssd_conv, state_ssm,
                                                    state_gdn_conv, state_gdn, wts)
    return (y_prompt, y_sample, p_sc, p_ssm, p_gc, p_gdn, s_sc, s_ssm, s_gc, s_gdn)
```

```python
import functools

import jax
import jax.numpy as jnp
from jax import lax
from jax.experimental import pallas as pl
from jax.experimental.pallas import tpu as pltpu

F32 = jnp.float32
BF16 = jnp.bfloat16

D_MODEL = 2048
CONV_WIDTH = 4
SSD_WIDTH = 1024
SSD_HEAD_DIM = 64
SSD_HEADS = 16
SSD_GROUPS = 2
SSD_STATE = 128
SSD_CONV_DIM = SSD_WIDTH + 2 * SSD_GROUPS * SSD_STATE
GDN_WIDTH = 1024
GDN_HEAD = 128
GDN_V_HEADS = 8
GDN_K_HEADS = 4
GDN_QK_DIM = GDN_K_HEADS * GDN_HEAD
GDN_CONV_DIM = 2 * GDN_QK_DIM + GDN_WIDTH
N_EXPERTS = 8
N_MOD = 6
EPS = 1e-6

PROJ_COLS = 5760
SMALL_BLOCK = 5632 // 128

V7X_VMEM_LIMIT = 56 * 1024 * 1024
TOKEN_TILE = 256
MM_TILE_M = 1024
FFN_TILE_M = 1024
FFN_TILE_F = 256
GATHER_TILE = 256
PROMPT_CHUNK = 128
NEG_BIG = -1e30


def _cparams(sem):
    return pltpu.CompilerParams(dimension_semantics=sem, vmem_limit_bytes=V7X_VMEM_LIMIT)


_NN = (((1,), (0,)), ((), ()))
_NT = (((1,), (1,)), ((), ()))
_TN = (((0,), (0,)), ((), ()))


def _dg(a, b, dims):
    return lax.dot_general(a, b, dims, preferred_element_type=F32)


def _dot(a, b, dims=_NN):
    return _dg(a.astype(BF16), b.astype(BF16), dims)


def _split(x, n):
    parts = []
    r = x
    for _ in range(n - 1):
        p = r.astype(BF16)
        parts.append(p)
        r = r - p.astype(F32)
    parts.append(r.astype(BF16))
    return parts


def _dot_exact_rhs(a, b, dims=_NN):
    bb = b.astype(BF16)
    a1, a2, a3 = _split(a, 3)
    return _dg(a3, bb, dims) + _dg(a2, bb, dims) + _dg(a1, bb, dims)


def _dot_exact_lhs(a, b, dims=_NN):
    ab = a.astype(BF16)
    b1, b2, b3 = _split(b, 3)
    return _dg(ab, b3, dims) + _dg(ab, b2, dims) + _dg(ab, b1, dims)


def _dot3(a, b, dims=_NN):
    a1, a2 = _split(a, 2)
    b1, b2 = _split(b, 2)
    return _dg(a1, b2, dims) + _dg(a2, b1, dims) + _dg(a1, b1, dims)


def _dot6(a, b, dims=_NN):
    a1, a2, a3 = _split(a, 3)
    b1, b2, b3 = _split(b, 3)
    small = _dg(a1, b3, dims) + _dg(a2, b2, dims) + _dg(a3, b1, dims)
    mid = _dg(a1, b2, dims) + _dg(a2, b1, dims)
    return small + mid + _dg(a1, b1, dims)


def _sigmoid(x):
    return 1.0 / (1.0 + jnp.exp(-x))


def _silu(x):
    return x * _sigmoid(x)


def _softplus(x):
    return jnp.maximum(x, 0.0) + jnp.log1p(jnp.exp(-jnp.abs(x)))


def _iota(shape, dim):
    return lax.broadcasted_iota(jnp.int32, shape, dim)


def _transpose_cols(x, n):
    eye = (_iota((n, n), 0) == _iota((n, n), 1)).astype(F32)
    return _dot_exact_lhs(eye, x, _NT)


def _conv_silu(u, ext_ref, s, w_ref, bias, q):
    ext_ref[s, 8:8 + q, :] = u
    acc = u * w_ref[CONV_WIDTH - 1:CONV_WIDTH, :]
    if bias is not None:
        acc = acc + bias
    for j in range(1, CONV_WIDTH):
        acc = acc + ext_ref[s, 8 - j:8 - j + q, :] * w_ref[CONV_WIDTH - 1 - j:CONV_WIDTH - j, :]
    return _silu(acc)


def _conv_advance(ext_ref, s, q):
    ext_ref[s, 0:8, :] = ext_ref[s, q:q + 8, :]


def _conv_init(ext_ref, s, state_ref):
    ext_ref[s, 0:8, :] = jnp.zeros((8, ext_ref.shape[-1]), F32)
    ext_ref[s, 8 - (CONV_WIDTH - 1):8, :] = state_ref[s]


def _conv_tail(ext_ref, s, q):
    return ext_ref[s, q + 8 - (CONV_WIDTH - 1):q + 8, :]


def _ada_kernel(c_ref, w_ref, b_ref, o_ref):
    a = _silu(c_ref[...]).astype(BF16)
    o_ref[0] = jnp.dot(a, w_ref[0].astype(BF16), preferred_element_type=F32) + b_ref[0]


def _ada_call(c_all, w_ada, b_ada):
    depth, d, n = w_ada.shape
    m = c_all.shape[0]
    tn = 1024
    return pl.pallas_call(
        _ada_kernel,
        grid=(depth, n // tn),
        in_specs=[
            pl.BlockSpec((m, d), lambda l, j: (0, 0)),
            pl.BlockSpec((1, d, tn), lambda l, j: (l, 0, j)),
            pl.BlockSpec((1, 1, tn), lambda l, j: (l, 0, j)),
        ],
        out_specs=pl.BlockSpec((1, m, tn), lambda l, j: (l, 0, j)),
        out_shape=jax.ShapeDtypeStruct((depth, m, n), F32),
        compiler_params=_cparams(("parallel", "parallel")),
        name="ada_mod",
    )(c_all, w_ada, b_ada.reshape(depth, 1, n))


def _ew_kernel(*refs, has_res, modded, out_x, n_prompt_tiles):
    refs = list(refs)
    x_ref = refs.pop(0)
    if has_res:
        y_ref, gp_ref, gs_ref = refs.pop(0), refs.pop(0), refs.pop(0)
    w_ref = refs.pop(0)
    if modded:
        scp_ref, shp_ref, scs_ref, shs_ref = refs.pop(0), refs.pop(0), refs.pop(0), refs.pop(0)
    if out_x:
        xo_ref = refs.pop(0)
    h_ref = refs.pop(0)
    i = pl.program_id(0)

    def run(prompt):
        x = x_ref[...]
        if has_res:
            g = gp_ref[0] if prompt else gs_ref[...]
            x = x + g * y_ref[...]
            if out_x:
                xo_ref[...] = x
        h = x * lax.rsqrt(jnp.mean(x * x, axis=-1, keepdims=True) + EPS) * w_ref[...]
        if modded:
            sc = scp_ref[0] if prompt else scs_ref[...]
            sh = shp_ref[0] if prompt else shs_ref[...]
            h = h * (1.0 + sc) + sh
        h_ref[...] = h.astype(h_ref.dtype)

    @pl.when(i < n_prompt_tiles)
    def _():
        run(True)

    @pl.when(i >= n_prompt_tiles)
    def _():
        run(False)


def _ew_call(x, w, *, t_prompt, l_prompt, res=None, mod=None, out_x=False, h_dtype=BF16):
    t, d = x.shape
    tm = TOKEN_TILE
    npt = t_prompt // tm
    per_seq = l_prompt // tm

    def pspec(chunk, mp):
        bp = mp.shape[0]
        return pl.BlockSpec((1, 1, d), lambda i: (jnp.minimum(i // per_seq, bp - 1), 0, chunk))

    def sspec(chunk):
        return pl.BlockSpec((tm, d), lambda i: (jnp.maximum(i - npt, 0), chunk))

    row = pl.BlockSpec((tm, d), lambda i: (i, 0))
    args, specs = [x], [row]
    if res is not None:
        y, gch, gmp, gms = res
        args += [y, gmp, gms]
        specs += [row, pspec(gch, gmp), sspec(gch)]
    args.append(w.reshape(1, d))
    specs.append(pl.BlockSpec((1, d), lambda i: (0, 0)))
    if mod is not None:
        scc, shc, mmp, mms = mod
        args += [mmp, mmp, mms, mms]
        specs += [pspec(scc, mmp), pspec(shc, mmp), sspec(scc), sspec(shc)]
    out_shape, out_specs = [], []
    if out_x:
        out_shape.append(jax.ShapeDtypeStruct((t, d), F32))
        out_specs.append(row)
    out_shape.append(jax.ShapeDtypeStruct((t, d), h_dtype))
    out_specs.append(row)
    kern = functools.partial(_ew_kernel, has_res=res is not None, modded=mod is not None,
                             out_x=out_x, n_prompt_tiles=npt)
    outs = pl.pallas_call(
        kern, grid=(t // tm,), in_specs=specs, out_specs=out_specs, out_shape=out_shape,
        compiler_params=_cparams(("parallel",)), name="token_norm",
    )(*args)
    return outs if out_x else outs[0]


def _mm_kernel(a_ref, w_ref, o_ref):
    o_ref[...] = jnp.dot(a_ref[...], w_ref[...].astype(BF16), preferred_element_type=F32)


def _mm_call(a, w, tn):
    m, k = a.shape
    n = w.shape[1]
    tm = min(MM_TILE_M, m)
    return pl.pallas_call(
        _mm_kernel,
        grid=(m // tm, n // tn),
        in_specs=[pl.BlockSpec((tm, k), lambda i, j: (i, 0)), pl.BlockSpec((k, tn), lambda i, j: (0, j))],
        out_specs=pl.BlockSpec((tm, tn), lambda i, j: (i, j)),
        out_shape=jax.ShapeDtypeStruct((m, n), F32),
        compiler_params=_cparams(("parallel", "parallel")),
        name="in_proj",
    )(a, w)


def _out_proj_kernel(ap1_ref, ap2_ref, as1_ref, as2_ref, w1_ref, w2_ref, o_ref, *, n_prompt_tiles):
    i = pl.program_id(0)

    def run(a1_ref, a2_ref):
        o_ref[...] = (jnp.dot(a1_ref[...], w1_ref[...], preferred_element_type=F32)
                      + jnp.dot(a2_ref[...], w2_ref[...], preferred_element_type=F32))

    @pl.when(i < n_prompt_tiles)
    def _():
        run(ap1_ref, ap2_ref)

    @pl.when(i >= n_prompt_tiles)
    def _():
        run(as1_ref, as2_ref)


def _out_proj_call(yp_ssd, yp_gdn, ys_ssd, ys_gdn, w_out):
    tp, k1 = yp_ssd.shape
    ts = ys_ssd.shape[0]
    k2 = yp_gdn.shape[1]
    n = w_out.shape[1]
    tm = MM_TILE_M
    while tp % tm or ts % tm:
        tm //= 2
    tn = 1024
    npt = tp // tm
    pmap = lambda i, j: (jnp.minimum(i, npt - 1), 0)
    smap = lambda i, j: (jnp.maximum(i - npt, 0), 0)
    return pl.pallas_call(
        functools.partial(_out_proj_kernel, n_prompt_tiles=npt),
        grid=((tp + ts) // tm, n // tn),
        in_specs=[pl.BlockSpec((tm, k1), pmap), pl.BlockSpec((tm, k2), pmap),
                  pl.BlockSpec((tm, k1), smap), pl.BlockSpec((tm, k2), smap),
                  pl.BlockSpec((k1, tn), lambda i, j: (0, j)),
                  pl.BlockSpec((k2, tn), lambda i, j: (k1 // k2, j))],
        out_specs=pl.BlockSpec((tm, tn), lambda i, j: (i, j)),
        out_shape=jax.ShapeDtypeStruct((tp + ts, n), F32),
        compiler_params=_cparams(("parallel", "parallel")),
        name="out_proj",
    )(yp_ssd, yp_gdn, ys_ssd, ys_gdn, w_out, w_out)


def _ssd_kernel(z_ref, xs_ref, bm_ref, cm_ref, sm_ref, wx_ref, wb_ref, wc_ref, bx_ref, bb_ref, bc_ref,
                dtb_ref, alog_ref, dexp_ref, nw_ref, cx_ref, cb_ref, cc_ref, st_ref,
                y_ref, cxo_ref, cbo_ref, cco_ref, sto_ref, extx, extb, extc, s_scr, *, q, nseq, nc):
    g = pl.program_id(1)
    c = pl.program_id(2)
    hg = SSD_HEADS // SSD_GROUPS
    wg = SSD_WIDTH // SSD_GROUPS
    p = SSD_HEAD_DIM

    li = _iota((q, q), 0)
    si = _iota((q, q), 1)
    tril = si <= li
    tri = tril.astype(F32)
    sel = (_iota((128, hg), 0) == g * hg + _iota((128, hg), 1)).astype(F32)
    expand = (_iota((hg, wg), 1) // p == _iota((hg, wg), 0)).astype(F32)
    expand_t = (_iota((wg, hg), 0) // p == _iota((wg, hg), 1)).astype(F32)
    lane = _iota((q, 128), 1)

    for s in range(nseq):
        rows = slice(s * q, (s + 1) * q)

        @pl.when(c == 0)
        def _():
            _conv_init(extx, s, cx_ref)
            _conv_init(extb, s, cb_ref)
            _conv_init(extc, s, cc_ref)
            s_scr[s] = st_ref[s]

        xs = _conv_silu(xs_ref[rows, :], extx, s, wx_ref, bx_ref[...], q)
        bm = _conv_silu(bm_ref[rows, :], extb, s, wb_ref, bb_ref[...], q)
        cm = _conv_silu(cm_ref[rows, :], extc, s, wc_ref, bc_ref[...], q)

        @pl.when(c == nc - 1)
        def _():
            cxo_ref[s] = _conv_tail(extx, s, q)
            cbo_ref[s] = _conv_tail(extb, s, q)
            cco_ref[s] = _conv_tail(extc, s, q)

        _conv_advance(extx, s, q)
        _conv_advance(extb, s, q)
        _conv_advance(extc, s, q)

        dt_raw = _dot_exact_rhs(sm_ref[rows, :], sel)
        dt = _softplus(dt_raw + dtb_ref[0])
        a = dt * (-jnp.exp(alog_ref[0]))
        a_cum = _dot_exact_lhs(tri, a)
        a_cum_t = _transpose_cols(a_cum, hg)
        dt_e = _dot_exact_rhs(dt, expand)
        acum_e = _dot_exact_rhs(a_cum, expand)
        xdt = xs * dt_e
        alast_e = acum_e[q - 1:q, :]
        xd = xdt * jnp.exp(alast_e - acum_e)

        cb = _dot(cm, bm, _NT)
        s_old = s_scr[s]
        y = _dot(cm, s_old, _NT) * jnp.exp(acum_e)

        pieces = []
        for k in range(hg // 2):
            ms = []
            for hh in (2 * k, 2 * k + 1):
                seg = a_cum[:, hh:hh + 1] - a_cum_t[hh:hh + 1, :]
                ms.append(cb * jnp.exp(jnp.where(tril, seg, NEG_BIG)))
            xp = xdt[:, 2 * p * k:2 * p * (k + 1)]
            top = jnp.where(lane < p, xp, 0.0)
            bot = jnp.where(lane >= p, xp, 0.0)
            if q % 128 == 0:
                piece = _dot(jnp.concatenate(ms, axis=1), jnp.concatenate([top, bot], axis=0))
            else:
                piece = _dot(ms[0], top) + _dot(ms[1], bot)
            pieces.append(piece)
        y = y + jnp.concatenate(pieces, axis=1)

        alast_b = jnp.broadcast_to(a_cum_t[:, q - 1:q], (hg, 128))
        dec_rows = jnp.exp(_dot_exact_lhs(expand_t, alast_b))
        s_scr[s] = dec_rows * s_old + _dot(xd, bm, _TN)

        y = y + dexp_ref[...] * xs
        y = y * _silu(z_ref[rows, :])
        y = y * lax.rsqrt(jnp.mean(y * y, axis=-1, keepdims=True) + EPS) * nw_ref[...]
        y_ref[rows, :] = y.astype(y_ref.dtype)

        @pl.when(c == nc - 1)
        def _():
            sto_ref[s] = s_scr[s]


def _ssd_call(proj, lw, conv_state, ssm_state, *, row0, bsz, seq, q, nseq):
    nc = seq // q
    r = nseq * q
    assert nseq == 1 or nc == 1
    rb0 = row0 // r
    wg = SSD_WIDTH // SSD_GROUPS
    n = SSD_STATE
    grid = (bsz // nseq, SSD_GROUPS, nc)

    def rowmap(col0):
        return lambda b, g, c: (rb0 + b * nc + c, col0 + g)

    cw = lw["w_conv_ssd"]
    cbias = lw["b_conv_ssd"].reshape(1, SSD_CONV_DIM)
    hg = SSD_HEADS // SSD_GROUPS
    in_specs = [
        pl.BlockSpec((r, wg), rowmap(2048 // wg)),
        pl.BlockSpec((r, wg), rowmap(4096 // wg)),
        pl.BlockSpec((r, n), rowmap(5120 // n)),
        pl.BlockSpec((r, n), rowmap(5120 // n + SSD_GROUPS)),
        pl.BlockSpec((r, 128), lambda b, g, c: (rb0 + b * nc + c, SMALL_BLOCK)),
        pl.BlockSpec((CONV_WIDTH, wg), lambda b, g, c: (0, g)),
        pl.BlockSpec((CONV_WIDTH, n), lambda b, g, c: (0, SSD_WIDTH // n + g)),
        pl.BlockSpec((CONV_WIDTH, n), lambda b, g, c: (0, SSD_WIDTH // n + SSD_GROUPS + g)),
        pl.BlockSpec((1, wg), lambda b, g, c: (0, g)),
        pl.BlockSpec((1, n), lambda b, g, c: (0, SSD_WIDTH // n + g)),
        pl.BlockSpec((1, n), lambda b, g, c: (0, SSD_WIDTH // n + SSD_GROUPS + g)),
        pl.BlockSpec((1, 1, hg), lambda b, g, c: (g, 0, 0)),
        pl.BlockSpec((1, 1, hg), lambda b, g, c: (g, 0, 0)),
        pl.BlockSpec((1, wg), lambda b, g, c: (0, g)),
        pl.BlockSpec((1, wg), lambda b, g, c: (0, g)),
        pl.BlockSpec((nseq, CONV_WIDTH - 1, wg), lambda b, g, c: (b, 0, g)),
        pl.BlockSpec((nseq, CONV_WIDTH - 1, n), lambda b, g, c: (b, 0, SSD_WIDTH // n + g)),
        pl.BlockSpec((nseq, CONV_WIDTH - 1, n), lambda b, g, c: (b, 0, SSD_WIDTH // n + SSD_GROUPS + g)),
        pl.BlockSpec((nseq, wg, n), lambda b, g, c: (b, g, 0)),
    ]
    args = [proj, proj, proj, proj, proj, cw, cw, cw, cbias, cbias, cbias,
            lw["ssd_dt_bias"].reshape(SSD_GROUPS, 1, hg), lw["ssd_a_log"].reshape(SSD_GROUPS, 1, hg),
            jnp.repeat(lw["ssd_d"], SSD_HEAD_DIM).reshape(1, SSD_WIDTH), lw["ssd_norm"].reshape(1, SSD_WIDTH),
            conv_state, conv_state, conv_state, ssm_state]
    out_shape = [
        jax.ShapeDtypeStruct((bsz * seq, SSD_WIDTH), BF16),
        jax.ShapeDtypeStruct((bsz, CONV_WIDTH - 1, SSD_WIDTH), F32),
        jax.ShapeDtypeStruct((bsz, CONV_WIDTH - 1, SSD_GROUPS * n), F32),
        jax.ShapeDtypeStruct((bsz, CONV_WIDTH - 1, SSD_GROUPS * n), F32),
        jax.ShapeDtypeStruct((bsz, SSD_WIDTH, n), F32),
    ]
    out_specs = [
        pl.BlockSpec((r, wg), lambda b, g, c: (b * nc + c, g)),
        pl.BlockSpec((nseq, CONV_WIDTH - 1, wg), lambda b, g, c: (b, 0, g)),
        pl.BlockSpec((nseq, CONV_WIDTH - 1, n), lambda b, g, c: (b, 0, g)),
        pl.BlockSpec((nseq, CONV_WIDTH - 1, n), lambda b, g, c: (b, 0, g)),
        pl.BlockSpec((nseq, wg, n), lambda b, g, c: (b, g, 0)),
    ]
    scratch = [pltpu.VMEM((nseq, q + 8, wg), F32), pltpu.VMEM((nseq, q + 8, n), F32),
               pltpu.VMEM((nseq, q + 8, n), F32), pltpu.VMEM((nseq, wg, n), F32)]
    kern = functools.partial(_ssd_kernel, q=q, nseq=nseq, nc=nc)
    y, cx, cb, cc, st = pl.pallas_call(
        kern, grid=grid, in_specs=in_specs, out_specs=out_specs, out_shape=out_shape,
        scratch_shapes=scratch,
        compiler_params=_cparams(("parallel", "parallel", "arbitrary")), name="ssd_mixer",
    )(*args)
    return y, jnp.concatenate([cx, cb, cc], axis=-1), st


def _neumann_inverse(x, nil):
    n = x.shape[0]
    eye = (_iota((n, n), 0) == _iota((n, n), 1)).astype(F32)
    inv = eye - x
    pw = x
    k = 2
    while k < nil:
        pw = _dot3(pw, pw)
        inv = inv + _dot3(inv, pw)
        k *= 2
    return inv


def _unit_lower_inverse(m, q):
    blk = 16
    if q <= blk:
        return _neumann_inverse(m, q)
    same = (_iota((q, q), 0) // blk) == (_iota((q, q), 1) // blk)
    dg = jnp.where(same, m, 0.0)
    dinv = _neumann_inverse(dg, blk)
    nn = _dot3(dinv, m - dg)
    ninv = _neumann_inverse(nn, q // blk)
    return _dot3(ninv, dinv)


def _gdn_kernel(q_ref, k_ref, v_ref, z_ref, sm_ref, wq_ref, wk_ref, wv_ref, dtb_ref, alog_ref, nw_ref,
                cq_ref, ck_ref, cv_ref, st_ref,
                y_ref, cqo_ref, cko_ref, cvo_ref, sto_ref, extq, extk, extv, s_scr, *, q, nseq, nc):
    kh = pl.program_id(1)
    c = pl.program_id(2)
    rep = GDN_V_HEADS // GDN_K_HEADS
    dk = GDN_HEAD
    dv = GDN_HEAD

    li = _iota((q, q), 0)
    si = _iota((q, q), 1)
    tril = si <= li
    strict = si < li
    tri = tril.astype(F32)
    ri = _iota((128, 8), 0)
    ci = _iota((128, 8), 1)
    a0 = SSD_HEADS
    b0 = SSD_HEADS + GDN_V_HEADS
    sel = (((ci < rep) & (ri == a0 + kh * rep + ci))
           | ((ci >= rep) & (ci < 2 * rep) & (ri == b0 + kh * rep + ci - rep))).astype(F32)

    for s in range(nseq):
        rows = slice(s * q, (s + 1) * q)

        @pl.when(c == 0)
        def _():
            _conv_init(extq, s, cq_ref)
            _conv_init(extk, s, ck_ref)
            _conv_init(extv, s, cv_ref)
            s_scr[s] = st_ref[s]

        qc = _conv_silu(q_ref[rows, :], extq, s, wq_ref, None, q)
        kc = _conv_silu(k_ref[rows, :], extk, s, wk_ref, None, q)
        vc = _conv_silu(v_ref[rows, :], extv, s, wv_ref, None, q)

        @pl.when(c == nc - 1)
        def _():
            cqo_ref[s] = _conv_tail(extq, s, q)
            cko_ref[s] = _conv_tail(extk, s, q)
            cvo_ref[s] = _conv_tail(extv, s, q)

        _conv_advance(extq, s, q)
        _conv_advance(extk, s, q)
        _conv_advance(extv, s, q)

        qn = qc * lax.rsqrt(jnp.sum(qc * qc, axis=-1, keepdims=True) + EPS) * (dk ** -0.5)
        kn = kc * lax.rsqrt(jnp.sum(kc * kc, axis=-1, keepdims=True) + EPS)

        sm = _dot_exact_rhs(sm_ref[rows, :], sel)
        gate = -jnp.exp(alog_ref[0]) * _softplus(sm + dtb_ref[0])
        g_cum = _dot_exact_lhs(tri, gate)
        g_cum_t = _transpose_cols(g_cum, 8)
        beta_all = _sigmoid(sm)

        kk = _dot(kn, kn, _NT)
        qk = _dot(qn, kn, _NT)

        outs = []
        for jj in range(rep):
            col = g_cum[:, jj:jj + 1]
            seg = col - g_cum_t[jj:jj + 1, :]
            decay = jnp.exp(jnp.where(tril, seg, NEG_BIG))
            beta = beta_all[:, rep + jj:rep + jj + 1]
            m = jnp.where(strict, beta * kk * decay, 0.0)
            t_inv = _unit_lower_inverse(m, q)
            e_col = jnp.exp(col)
            vj = vc[:, dv * jj:dv * (jj + 1)]
            rhs = jnp.concatenate([vj * beta, kn * (beta * e_col)], axis=1)
            sol = _dot3(t_inv, rhs)
            u = sol[:, :dv]
            w = sol[:, dv:]
            attn = qk * decay
            g_last = g_cum[q - 1:q, jj:jj + 1]
            q_dec = qn * e_col
            k_dec = kn * jnp.exp(g_last - col)
            st = s_scr[s, dk * jj:dk * (jj + 1), :]
            both = _dot(jnp.concatenate([w, q_dec], axis=0), st)
            ws, qs = both[:q], both[q:]
            v_new = u - ws
            o = qs + _dot(attn, v_new)
            s_scr[s, dk * jj:dk * (jj + 1), :] = st * jnp.exp(g_last) + _dot(k_dec, v_new, _TN)
            o = o * lax.rsqrt(jnp.mean(o * o, axis=-1, keepdims=True) + EPS) * nw_ref[...]
            o = o * _silu(z_ref[rows, dv * jj:dv * (jj + 1)])
            outs.append(o)
        y_ref[rows, :] = jnp.concatenate(outs, axis=1).astype(y_ref.dtype)

        @pl.when(c == nc - 1)
        def _():
            sto_ref[s] = s_scr[s]


def _gdn_call(proj, lw, conv_state, gdn_state, *, row0, bsz, seq, q, nseq):
    nc = seq // q
    r = nseq * q
    assert nseq == 1 or nc == 1
    rb0 = row0 // r
    rep = GDN_V_HEADS // GDN_K_HEADS
    hd = GDN_HEAD
    vw = rep * hd
    grid = (bsz // nseq, GDN_K_HEADS, nc)
    rowi = lambda b, kh, c: rb0 + b * nc + c

    def pad8(v):
        v = v.reshape(GDN_K_HEADS, 1, rep)
        return jnp.concatenate([v, jnp.zeros((GDN_K_HEADS, 1, 8 - rep), F32)], axis=-1)

    cw = lw["w_conv_gdn"]
    in_specs = [
        pl.BlockSpec((r, hd), lambda b, kh, c: (rowi(b, kh, c), kh)),
        pl.BlockSpec((r, hd), lambda b, kh, c: (rowi(b, kh, c), GDN_QK_DIM // hd + kh)),
        pl.BlockSpec((r, vw), lambda b, kh, c: (rowi(b, kh, c), 2 * GDN_QK_DIM // vw + kh)),
        pl.BlockSpec((r, vw), lambda b, kh, c: (rowi(b, kh, c), 3072 // vw + kh)),
        pl.BlockSpec((r, 128), lambda b, kh, c: (rowi(b, kh, c), SMALL_BLOCK)),
        pl.BlockSpec((CONV_WIDTH, hd), lambda b, kh, c: (0, kh)),
        pl.BlockSpec((CONV_WIDTH, hd), lambda b, kh, c: (0, GDN_QK_DIM // hd + kh)),
        pl.BlockSpec((CONV_WIDTH, vw), lambda b, kh, c: (0, 2 * GDN_QK_DIM // vw + kh)),
        pl.BlockSpec((1, 1, 8), lambda b, kh, c: (kh, 0, 0)),
        pl.BlockSpec((1, 1, 8), lambda b, kh, c: (kh, 0, 0)),
        pl.BlockSpec((1, hd), lambda b, kh, c: (0, 0)),
        pl.BlockSpec((nseq, CONV_WIDTH - 1, hd), lambda b, kh, c: (b, 0, kh)),
        pl.BlockSpec((nseq, CONV_WIDTH - 1, hd), lambda b, kh, c: (b, 0, GDN_QK_DIM // hd + kh)),
        pl.BlockSpec((nseq, CONV_WIDTH - 1, vw), lambda b, kh, c: (b, 0, 2 * GDN_QK_DIM // vw + kh)),
        pl.BlockSpec((nseq, vw, hd), lambda b, kh, c: (b, kh, 0)),
    ]
    args = [proj, proj, proj, proj, proj, cw, cw, cw, pad8(lw["gdn_dt_bias"]), pad8(lw["gdn_a_log"]),
            lw["gdn_norm"].reshape(1, hd), conv_state, conv_state, conv_state, gdn_state]
    out_shape = [
        jax.ShapeDtypeStruct((bsz * seq, GDN_WIDTH), BF16),
        jax.ShapeDtypeStruct((bsz, CONV_WIDTH - 1, GDN_QK_DIM), F32),
        jax.ShapeDtypeStruct((bsz, CONV_WIDTH - 1, GDN_QK_DIM), F32),
        jax.ShapeDtypeStruct((bsz, CONV_WIDTH - 1, GDN_WIDTH), F32),
        jax.ShapeDtypeStruct((bsz, GDN_V_HEADS * hd, hd), F32),
    ]
    out_specs = [
        pl.BlockSpec((r, vw), lambda b, kh, c: (b * nc + c, kh)),
        pl.BlockSpec((nseq, CONV_WIDTH - 1, hd), lambda b, kh, c: (b, 0, kh)),
        pl.BlockSpec((nseq, CONV_WIDTH - 1, hd), lambda b, kh, c: (b, 0, kh)),
        pl.BlockSpec((nseq, CONV_WIDTH - 1, vw), lambda b, kh, c: (b, 0, kh)),
        pl.BlockSpec((nseq, vw, hd), lambda b, kh, c: (b, kh, 0)),
    ]
    scratch = [pltpu.VMEM((nseq, q + 8, hd), F32), pltpu.VMEM((nseq, q + 8, hd), F32),
               pltpu.VMEM((nseq, q + 8, vw), F32), pltpu.VMEM((nseq, vw, hd), F32)]
    kern = functools.partial(_gdn_kernel, q=q, nseq=nseq, nc=nc)
    y, cq, ck, cv, st = pl.pallas_call(
        kern, grid=grid, in_specs=in_specs, out_specs=out_specs, out_shape=out_shape,
        scratch_shapes=scratch,
        compiler_params=_cparams(("parallel", "parallel", "arbitrary")), name="gdn_mixer",
    )(*args)
    return y, jnp.concatenate([cq, ck, cv], axis=-1), st


def _ffn_kernel(te_ref, xi_ref, nv_ref, x_ref, wg_ref, wu_ref, wd_ref, o_ref, acc_ref):
    i = pl.program_id(0)
    j = pl.program_id(1)
    valid = i < nv_ref[0]

    @pl.when(valid)
    def _():
        x = x_ref[...]
        gate = jnp.dot(x, wg_ref[0].astype(BF16), preferred_element_type=F32)
        up = jnp.dot(x, wu_ref[0].astype(BF16), preferred_element_type=F32)
        act = (_silu(gate) * up).astype(BF16)
        part = jnp.dot(act, wd_ref[0].astype(BF16), preferred_element_type=F32)

        @pl.when(j == 0)
        def _():
            acc_ref[...] = part

        @pl.when(j > 0)
        def _():
            acc_ref[...] += part

    last = j == pl.num_programs(1) - 1

    @pl.when(last & valid)
    def _():
        o_ref[...] = acc_ref[...]

    @pl.when(last & jnp.logical_not(valid))
    def _():
        o_ref[...] = jnp.zeros(o_ref.shape, o_ref.dtype)


def _ffn_call(x, w_gate, w_up, w_down, tile_expert, tile_src, n_valid):
    tp, d = x.shape
    f = w_gate.shape[2]
    tm, tf = FFN_TILE_M, FFN_TILE_F
    nf = f // tf
    nt = tp // tm

    def jeff(i, j, nv):
        return jnp.where(i < nv[0], j, nf - 1)

    grid_spec = pltpu.PrefetchScalarGridSpec(
        num_scalar_prefetch=3,
        grid=(nt, nf),
        in_specs=[
            pl.BlockSpec((tm, d), lambda i, j, te, xi, nv: (xi[i], 0)),
            pl.BlockSpec((1, d, tf), lambda i, j, te, xi, nv: (te[i], 0, jeff(i, j, nv))),
            pl.BlockSpec((1, d, tf), lambda i, j, te, xi, nv: (te[i], 0, jeff(i, j, nv))),
            pl.BlockSpec((1, tf, d), lambda i, j, te, xi, nv: (te[i], jeff(i, j, nv), 0)),
        ],
        out_specs=pl.BlockSpec((tm, d), lambda i, j, te, xi, nv: (i, 0)),
        scratch_shapes=[pltpu.VMEM((tm, d), F32)],
    )
    return pl.pallas_call(
        _ffn_kernel, grid_spec=grid_spec, out_shape=jax.ShapeDtypeStruct((tp, d), F32),
        compiler_params=_cparams(("arbitrary", "arbitrary")), name="grouped_swiglu",
    )(tile_expert, tile_src, n_valid, x, w_gate, w_up, w_down)


def _router_kernel(h_ref, wr_ref, sel_ref, cw_ref):
    logits = _dot6(wr_ref[...], h_ref[...], _NT)
    ne = logits.shape[0]
    ei = _iota(logits.shape, 0)
    m1 = jnp.max(logits, axis=0, keepdims=True)
    i1 = jnp.min(jnp.where(logits == m1, ei, ne), axis=0, keepdims=True)
    rest = jnp.where(ei == i1, -jnp.inf, logits)
    m2 = jnp.max(rest, axis=0, keepdims=True)
    i2 = jnp.min(jnp.where(rest == m2, ei, ne), axis=0, keepdims=True)
    e = jnp.exp(m2 - m1)
    w1 = 1.0 / (1.0 + e)
    w2 = e / (1.0 + e)
    sel_ref[...] = jnp.where(ei == i1, 1, jnp.where(ei == i2, 2, 0)).astype(jnp.int32)
    cw_ref[...] = jnp.where(ei == i1, w1, jnp.where(ei == i2, w2, 0.0))


def _router_call(h, w_router):
    t, d = h.shape
    ne = w_router.shape[1]
    tm = TOKEN_TILE
    return pl.pallas_call(
        _router_kernel, grid=(t // tm,),
        in_specs=[pl.BlockSpec((tm, d), lambda i: (i, 0)), pl.BlockSpec((ne, d), lambda i: (0, 0))],
        out_specs=[pl.BlockSpec((ne, tm), lambda i: (0, i)), pl.BlockSpec((ne, tm), lambda i: (0, i))],
        out_shape=[jax.ShapeDtypeStruct((ne, t), jnp.int32), jax.ShapeDtypeStruct((ne, t), F32)],
        compiler_params=_cparams(("parallel",)), name="router_top2",
    )(h, w_router.T)


def _gather_kernel(*refs, n_src, weighted, tm):
    idx_now = refs[:n_src]
    idx_next = refs[n_src:2 * n_src]
    refs = refs[2 * n_src:]
    table = refs[0]
    refs = refs[1:]
    if weighted:
        w_ref = refs[0]
        refs = refs[1:]
    o_ref, buf, sem = refs
    i = pl.program_id(0)
    n = pl.num_programs(0)

    def row_copy(idx_ref, k, slot, r):
        return pltpu.make_async_copy(table.at[pl.ds(idx_ref[0, 0, r], 1), :],
                                     buf.at[slot, k, pl.ds(r, 1), :], sem.at[slot, k])

    def issue(idx_refs, slot):
        for k in range(n_src):
            def body(r, carry, k=k):
                row_copy(idx_refs[k], k, slot, r).start()
                return carry
            lax.fori_loop(0, tm, body, 0)

    slot = lax.rem(i, 2)

    @pl.when(i == 0)
    def _():
        issue(idx_now, 0)

    @pl.when(i + 1 < n)
    def _():
        issue(idx_next, 1 - slot)

    for k in range(n_src):
        def wbody(r, carry, k=k):
            row_copy(idx_now[k], k, slot, r).wait()
            return carry
        lax.fori_loop(0, tm, wbody, 0)

    if weighted:
        acc = buf[slot, 0] * w_ref[:, 0:1]
        for k in range(1, n_src):
            acc = acc + buf[slot, k] * w_ref[:, k:k + 1]
        o_ref[...] = acc.astype(o_ref.dtype)
    else:
        o_ref[...] = buf[slot, 0].astype(o_ref.dtype)


def _gather_call(table, idx, weights, out_dtype):
    n_src, t_out = idx.shape
    d = table.shape[1]
    tm = GATHER_TILE
    nt = t_out // tm
    idx3 = idx.reshape(n_src, nt, 1, tm)
    smem = functools.partial(pl.BlockSpec, memory_space=pltpu.SMEM)
    in_specs = ([smem((1, 1, tm), lambda i: (i, 0, 0)) for _ in range(n_src)]
                + [smem((1, 1, tm), lambda i: (jnp.minimum(i + 1, nt - 1), 0, 0)) for _ in range(n_src)]
                + [pl.BlockSpec(memory_space=pl.ANY)])
    args = [idx3[k] for k in range(n_src)] * 2 + [table]
    if weights is not None:
        in_specs.append(pl.BlockSpec((tm, n_src), lambda i: (i, 0)))
        args.append(weights)
    kern = functools.partial(_gather_kernel, n_src=n_src, weighted=weights is not None, tm=tm)
    return pl.pallas_call(
        kern, grid=(nt,), in_specs=in_specs,
        out_specs=pl.BlockSpec((tm, d), lambda i: (i, 0)),
        out_shape=jax.ShapeDtypeStruct((t_out, d), out_dtype),
        scratch_shapes=[pltpu.VMEM((2, n_src, tm, d), table.dtype), pltpu.SemaphoreType.DMA((2, n_src))],
        compiler_params=_cparams(("arbitrary",)), name="row_gather",
    )(*args)


def _moe(h, w_router, w_gate, w_up, w_down):
    t, d = h.shape
    ne = w_router.shape[1]
    tm = FFN_TILE_M
    sel, cw = _router_call(h, w_router)
    hot = (sel > 0).astype(jnp.int32)
    rank = jnp.cumsum(hot, axis=1) - hot
    counts = jnp.sum(hot, axis=1)
    tiles_e = (counts + tm - 1) // tm
    tile_end = jnp.cumsum(tiles_e)
    tile_start = tile_end - tiles_e
    n_valid = tile_end[-1]
    n_tiles = (2 * t + tm - 1) // tm + ne
    tp = n_tiles * tm
    dest = tile_start[:, None] * tm + rank
    tok = jnp.broadcast_to(jnp.arange(t, dtype=jnp.int32)[None, :], (ne, t))
    src = jnp.zeros((tp + 1,), jnp.int32).at[jnp.where(hot > 0, dest, tp).reshape(-1)].set(tok.reshape(-1))[:tp]
    tid = jnp.arange(n_tiles, dtype=jnp.int32)
    te = jnp.sum((tid[:, None] >= tile_end[None, :]).astype(jnp.int32), axis=1)
    last = jnp.maximum(n_valid - 1, 0)
    te = jnp.where(tid < n_valid, te, te[last]).astype(jnp.int32)
    xi = jnp.minimum(tid, last).astype(jnp.int32)
    p1 = jnp.sum(jnp.where(sel == 1, dest, 0), axis=0)
    p2 = jnp.sum(jnp.where(sel == 2, dest, 0), axis=0)
    wts = jnp.stack([jnp.sum(jnp.where(sel == 1, cw, 0.0), axis=0),
                     jnp.sum(jnp.where(sel == 2, cw, 0.0), axis=0)], axis=1)

    xs = _gather_call(h, src[None, :], None, BF16)
    ys = _ffn_call(xs, w_gate, w_up, w_down, te, xi, n_valid.reshape(1).astype(jnp.int32))
    return _gather_call(ys, jnp.stack([p1, p2]).astype(jnp.int32), wts, F32)


def _reorder_in_proj(w_in):
    o = 0
    seg = {}
    for name, width in (("z_ssd", SSD_WIDTH), ("xs", SSD_WIDTH), ("bc", SSD_CONV_DIM - SSD_WIDTH),
                        ("dt", SSD_HEADS), ("qkv", GDN_CONV_DIM), ("z_gdn", GDN_WIDTH),
                        ("a", GDN_V_HEADS), ("b", GDN_V_HEADS)):
        seg[name] = w_in[:, :, o:o + width]
        o += width
    pad = jnp.zeros(w_in.shape[:2] + (128 - SSD_HEADS - 2 * GDN_V_HEADS,), w_in.dtype)
    parts = [seg["qkv"], seg["z_ssd"], seg["z_gdn"], seg["xs"], seg["bc"], seg["dt"], seg["a"], seg["b"], pad]
    return jnp.concatenate(parts, axis=-1).astype(BF16)


def kernel(x_prompt, x_sample, c_prompt, c_sample, state_ssd_conv, state_ssm, state_gdn_conv, state_gdn, w_ada, b_ada, w_norm_mix, w_norm_ffn, w_in, w_conv_ssd, b_conv_ssd, ssd_dt_bias, ssd_a_log, ssd_d, ssd_norm, w_conv_gdn, gdn_dt_bias, gdn_a_log, gdn_norm, w_out, w_ffn_gate, w_ffn_up, w_ffn_down, w_router, w_exp_gate, w_exp_up, w_exp_down, w_norm_final):
    bp, lp, d = x_prompt.shape
    bs, ls, _ = x_sample.shape
    depth = w_in.shape[0]
    tpr = bp * lp
    tsa = bs * ls
    t = tpr + tsa

    x = jnp.concatenate([x_prompt.reshape(tpr, d), x_sample.reshape(tsa, d)], axis=0)
    mod = _ada_call(jnp.concatenate([c_sample, c_prompt], axis=0), w_ada, b_ada)
    w_in_r = _reorder_in_proj(w_in)
    w_out_b = w_out.astype(BF16)

    qp = PROMPT_CHUNK if lp % PROMPT_CHUNK == 0 else lp
    nseq_s = 16 // ls if ls < 16 else 1
    f32 = F32
    zeros_p = (jnp.zeros((bp, CONV_WIDTH - 1, SSD_CONV_DIM), f32), jnp.zeros((bp, SSD_WIDTH, SSD_STATE), f32),
               jnp.zeros((bp, CONV_WIDTH - 1, GDN_CONV_DIM), f32), jnp.zeros((bp, GDN_V_HEADS * GDN_HEAD, GDN_HEAD), f32))

    new_p = [[], [], [], []]
    new_s = [[], [], [], []]
    common = dict(t_prompt=tpr, l_prompt=lp)
    mods = [(mod[l, bs:].reshape(bp, 1, N_MOD * d), jnp.repeat(mod[l, :bs], ls, axis=0)) for l in range(depth)]
    h = _ew_call(x, w_norm_mix[0], mod=(1, 0) + mods[0], **common)
    for l in range(depth):
        mod_p, mod_s = mods[l]
        lw = {"w_conv_ssd": w_conv_ssd[l], "b_conv_ssd": b_conv_ssd[l], "ssd_dt_bias": ssd_dt_bias[l],
              "ssd_a_log": ssd_a_log[l], "ssd_d": ssd_d[l], "ssd_norm": ssd_norm[l],
              "w_conv_gdn": w_conv_gdn[l], "gdn_dt_bias": gdn_dt_bias[l], "gdn_a_log": gdn_a_log[l],
              "gdn_norm": gdn_norm[l]}
        proj = _mm_call(h, w_in_r[l], 640)

        prompt = dict(row0=0, bsz=bp, seq=lp, q=qp, nseq=1)
        sample = dict(row0=tpr, bsz=bs, seq=ls, q=ls, nseq=nseq_s)
        yp_ssd, pc, pst = _ssd_call(proj, lw, zeros_p[0], zeros_p[1], **prompt)
        ys_ssd, sc_, sst = _ssd_call(proj, lw, state_ssd_conv[l], state_ssm[l].reshape(bs, SSD_WIDTH, SSD_STATE),
                                     **sample)
        yp_gdn, pgc, pgst = _gdn_call(proj, lw, zeros_p[2], zeros_p[3], **prompt)
        ys_gdn, sgc, sgst = _gdn_call(proj, lw, state_gdn_conv[l],
                                      state_gdn[l].reshape(bs, GDN_V_HEADS * GDN_HEAD, GDN_HEAD), **sample)
        for lst, v in zip(new_p, (pc, pst, pgc, pgst)):
            lst.append(v)
        for lst, v in zip(new_s, (sc_, sst, sgc, sgst)):
            lst.append(v)

        mix = _out_proj_call(yp_ssd, yp_gdn, ys_ssd, ys_gdn, w_out_b[l])
        moe_layer = l % 2 == 1
        x, h2 = _ew_call(x, w_norm_ffn[l], res=(mix, 2, mod_p, mod_s), mod=(4, 3, mod_p, mod_s), out_x=True,
                         h_dtype=F32 if moe_layer else BF16, **common)
        i = l // 2
        if moe_layer:
            f = _moe(h2, w_router[i], w_exp_gate[i], w_exp_up[i], w_exp_down[i])
        else:
            nt = t // FFN_TILE_M
            tid = jnp.arange(nt, dtype=jnp.int32)
            f = _ffn_call(h2, w_ffn_gate[i][None], w_ffn_up[i][None], w_ffn_down[i][None],
                          jnp.zeros((nt,), jnp.int32), tid, jnp.full((1,), nt, jnp.int32))
        if l + 1 < depth:
            x, h = _ew_call(x, w_norm_mix[l + 1], res=(f, 5, mod_p, mod_s), mod=(1, 0) + mods[l + 1],
                            out_x=True, **common)
        else:
            y = _ew_call(x, w_norm_final, res=(f, 5, mod_p, mod_s), h_dtype=F32, **common)

    def stack_states(lst, bsz):
        conv_s = jnp.stack(lst[0])
        ssm = jnp.stack(lst[1]).reshape(depth, bsz, SSD_HEADS, SSD_HEAD_DIM, SSD_STATE)
        conv_g = jnp.stack(lst[2])
        gdn = jnp.stack(lst[3]).reshape(depth, bsz, GDN_V_HEADS, GDN_HEAD, GDN_HEAD)
        return conv_s, ssm, conv_g, gdn

    return ((y[:tpr].reshape(bp, lp, d), y[tpr:].reshape(bs, ls, d))
            + stack_states(new_p, bp) + stack_states(new_s, bs))
```

```python
import functools

import jax
import jax.numpy as jnp
from jax import lax
from jax.experimental import pallas as pl
from jax.experimental.pallas import tpu as pltpu

F32 = jnp.float32
BF16 = jnp.bfloat16

D_MODEL = 2048
CONV_WIDTH = 4
SSD_WIDTH = 1024
SSD_HEAD_DIM = 64
SSD_HEADS = 16
SSD_GROUPS = 2
SSD_STATE = 128
SSD_CONV_DIM = SSD_WIDTH + 2 * SSD_GROUPS * SSD_STATE
GDN_WIDTH = 1024
GDN_HEAD = 128
GDN_V_HEADS = 8
GDN_K_HEADS = 4
GDN_QK_DIM = GDN_K_HEADS * GDN_HEAD
GDN_CONV_DIM = 2 * GDN_QK_DIM + GDN_WIDTH
N_EXPERTS = 8
N_MOD = 6
EPS = 1e-6

PROJ_COLS = 5760
COL_QKV, COL_Z_SSD, COL_Z_GDN, COL_X_SSD, COL_BC, COL_SMALL = 0, 2048, 3072, 4096, 5120, 5632
SMALL_DT, SMALL_A, SMALL_B = 0, SSD_HEADS, SSD_HEADS + GDN_V_HEADS

V7X_VMEM_LIMIT = 56 * 1024 * 1024
TOKEN_TILE = 256
MM_TILE_M = 1024
FFN_TILE_M = 1024
FFN_TILE_F = 256
GATHER_TILE = 256
GATHER_UNROLL = 8
PROMPT_CHUNK = 128
NEG_BIG = -1e30


def _cparams(sem):
    return pltpu.CompilerParams(dimension_semantics=sem, vmem_limit_bytes=V7X_VMEM_LIMIT)


_NN = (((1,), (0,)), ((), ()))
_NT = (((1,), (1,)), ((), ()))
_TN = (((0,), (0,)), ((), ()))


def _dg(a, b, dims):
    return lax.dot_general(a, b, dims, preferred_element_type=F32)


def _dot(a, b, dims=_NN):
    return _dg(a.astype(BF16), b.astype(BF16), dims)


def _split(x, n):
    parts = []
    r = x
    for _ in range(n - 1):
        p = r.astype(BF16)
        parts.append(p)
        r = r - p.astype(F32)
    parts.append(r.astype(BF16))
    return parts


def _dot_exact_rhs(a, b, dims=_NN):
    bb = b.astype(BF16)
    a1, a2, a3 = _split(a, 3)
    return _dg(a3, bb, dims) + _dg(a2, bb, dims) + _dg(a1, bb, dims)


def _dot_exact_lhs(a, b, dims=_NN):
    ab = a.astype(BF16)
    b1, b2, b3 = _split(b, 3)
    return _dg(ab, b3, dims) + _dg(ab, b2, dims) + _dg(ab, b1, dims)


def _dot6(a, b, dims=_NN):
    a1, a2, a3 = _split(a, 3)
    b1, b2, b3 = _split(b, 3)
    small = _dg(a1, b3, dims) + _dg(a2, b2, dims) + _dg(a3, b1, dims)
    mid = _dg(a1, b2, dims) + _dg(a2, b1, dims)
    return small + mid + _dg(a1, b1, dims)


def _sigmoid(x):
    return 1.0 / (1.0 + jnp.exp(-x))


def _silu(x):
    return x * _sigmoid(x)


def _softplus(x):
    return jnp.maximum(x, 0.0) + jnp.log1p(jnp.exp(-jnp.abs(x)))


def _iota(shape, dim):
    return lax.broadcasted_iota(jnp.int32, shape, dim)


def _transpose_cols(x, n):
    eye = (_iota((n, n), 0) == _iota((n, n), 1)).astype(F32)
    return _dot_exact_lhs(eye, x, _NT)


def _conv_silu(u, ext_ref, s, w_ref, bias, q):
    ext_ref[s, 8:8 + q, :] = u
    acc = u * w_ref[CONV_WIDTH - 1:CONV_WIDTH, :]
    if bias is not None:
        acc = acc + bias
    for j in range(1, CONV_WIDTH):
        acc = acc + ext_ref[s, 8 - j:8 - j + q, :] * w_ref[CONV_WIDTH - 1 - j:CONV_WIDTH - j, :]
    return _silu(acc)


def _conv_advance(ext_ref, s, q):
    ext_ref[s, 0:8, :] = ext_ref[s, q:q + 8, :]


def _conv_init(ext_ref, s, state_rows):
    ext_ref[s, 0:8, :] = jnp.zeros((8, ext_ref.shape[-1]), F32)
    ext_ref[s, 8 - (CONV_WIDTH - 1):8, :] = state_rows


def _conv_tail(ext_ref, s, q):
    return ext_ref[s, q + 8 - (CONV_WIDTH - 1):q + 8, :]


def _ada_kernel(c_ref, w_ref, b_ref, o_ref):
    a = _silu(c_ref[...]).astype(BF16)
    o_ref[0] = jnp.dot(a, w_ref[0].astype(BF16), preferred_element_type=F32) + b_ref[0]


def _ada_call(c_all, w_ada, b_ada):
    depth, d, n = w_ada.shape
    m = c_all.shape[0]
    tn = 1024
    return pl.pallas_call(
        _ada_kernel,
        grid=(depth, n // tn),
        in_specs=[
            pl.BlockSpec((m, d), lambda l, j: (0, 0)),
            pl.BlockSpec((1, d, tn), lambda l, j: (l, 0, j)),
            pl.BlockSpec((1, 1, tn), lambda l, j: (l, 0, j)),
        ],
        out_specs=pl.BlockSpec((1, m, tn), lambda l, j: (l, 0, j)),
        out_shape=jax.ShapeDtypeStruct((depth, m, n), F32),
        compiler_params=_cparams(("parallel", "parallel")),
        name="ada_mod",
    )(c_all, w_ada, b_ada.reshape(depth, 1, n))


def _ew_kernel(*refs, has_res, modded, out_x, n_prompt_tiles):
    refs = list(refs)
    x_ref = refs.pop(0)
    if has_res:
        y_ref, gp_ref, gs_ref = refs.pop(0), refs.pop(0), refs.pop(0)
    w_ref = refs.pop(0)
    if modded:
        scp_ref, shp_ref, scs_ref, shs_ref = refs.pop(0), refs.pop(0), refs.pop(0), refs.pop(0)
    if out_x:
        xo_ref = refs.pop(0)
    h_ref = refs.pop(0)
    i = pl.program_id(0)

    def run(prompt):
        x = x_ref[...]
        if has_res:
            g = gp_ref[0] if prompt else gs_ref[...]
            x = x + g * y_ref[...]
            if out_x:
                xo_ref[...] = x
        h = x * lax.rsqrt(jnp.mean(x * x, axis=-1, keepdims=True) + EPS) * w_ref[...]
        if modded:
            sc = scp_ref[0] if prompt else scs_ref[...]
            sh = shp_ref[0] if prompt else shs_ref[...]
            h = h * (1.0 + sc) + sh
        h_ref[...] = h.astype(h_ref.dtype)

    @pl.when(i < n_prompt_tiles)
    def _():
        run(True)

    @pl.when(i >= n_prompt_tiles)
    def _():
        run(False)


def _ew_call(x, w, *, t_prompt, l_prompt, res=None, mod=None, out_x=False, h_dtype=BF16):
    t, d = x.shape
    tm = TOKEN_TILE
    npt = t_prompt // tm
    per_seq = l_prompt // tm

    def pspec(chunk, mp):
        bp = mp.shape[0]
        return pl.BlockSpec((1, 1, d), lambda i: (jnp.minimum(i // per_seq, bp - 1), 0, chunk))

    def sspec(chunk):
        return pl.BlockSpec((tm, d), lambda i: (jnp.maximum(i - npt, 0), chunk))

    row = pl.BlockSpec((tm, d), lambda i: (i, 0))
    args, specs = [x], [row]
    if res is not None:
        y, gch, gmp, gms = res
        args += [y, gmp, gms]
        specs += [row, pspec(gch, gmp), sspec(gch)]
    args.append(w.reshape(1, d))
    specs.append(pl.BlockSpec((1, d), lambda i: (0, 0)))
    if mod is not None:
        scc, shc, mmp, mms = mod
        args += [mmp, mmp, mms, mms]
        specs += [pspec(scc, mmp), pspec(shc, mmp), sspec(scc), sspec(shc)]
    out_shape, out_specs = [], []
    if out_x:
        out_shape.append(jax.ShapeDtypeStruct((t, d), F32))
        out_specs.append(row)
    out_shape.append(jax.ShapeDtypeStruct((t, d), h_dtype))
    out_specs.append(row)
    kern = functools.partial(_ew_kernel, has_res=res is not None, modded=mod is not None,
                             out_x=out_x, n_prompt_tiles=npt)
    outs = pl.pallas_call(
        kern, grid=(t // tm,), in_specs=specs, out_specs=out_specs, out_shape=out_shape,
        compiler_params=_cparams(("parallel",)), name="token_norm",
    )(*args)
    return outs if out_x else outs[0]


def _mm_kernel(a_ref, w_ref, o_ref):
    o_ref[...] = jnp.dot(a_ref[...], w_ref[...].astype(BF16), preferred_element_type=F32)


def _mm_call(a, w, tn):
    m, k = a.shape
    n = w.shape[1]
    tm = min(MM_TILE_M, m)
    return pl.pallas_call(
        _mm_kernel,
        grid=(m // tm, n // tn),
        in_specs=[pl.BlockSpec((tm, k), lambda i, j: (i, 0)), pl.BlockSpec((k, tn), lambda i, j: (0, j))],
        out_specs=pl.BlockSpec((tm, tn), lambda i, j: (i, j)),
        out_shape=jax.ShapeDtypeStruct((m, n), F32),
        compiler_params=_cparams(("parallel", "parallel")),
        name="in_proj",
    )(a, w)


def _out_proj_kernel(ap1_ref, ap2_ref, as1_ref, as2_ref, w1_ref, w2_ref, o_ref, *, n_prompt_tiles):
    i = pl.program_id(0)

    def run(a1_ref, a2_ref):
        o_ref[...] = (jnp.dot(a1_ref[...], w1_ref[...], preferred_element_type=F32)
                      + jnp.dot(a2_ref[...], w2_ref[...], preferred_element_type=F32))

    @pl.when(i < n_prompt_tiles)
    def _():
        run(ap1_ref, ap2_ref)

    @pl.when(i >= n_prompt_tiles)
    def _():
        run(as1_ref, as2_ref)


def _out_proj_call(yp_ssd, yp_gdn, ys_ssd, ys_gdn, w_out):
    tp, k1 = yp_ssd.shape
    ts = ys_ssd.shape[0]
    k2 = yp_gdn.shape[1]
    n = w_out.shape[1]
    tm = MM_TILE_M
    while tp % tm or ts % tm:
        tm //= 2
    tn = 1024
    npt = tp // tm
    pmap = lambda i, j: (jnp.minimum(i, npt - 1), 0)
    smap = lambda i, j: (jnp.maximum(i - npt, 0), 0)
    return pl.pallas_call(
        functools.partial(_out_proj_kernel, n_prompt_tiles=npt),
        grid=((tp + ts) // tm, n // tn),
        in_specs=[pl.BlockSpec((tm, k1), pmap), pl.BlockSpec((tm, k2), pmap),
                  pl.BlockSpec((tm, k1), smap), pl.BlockSpec((tm, k2), smap),
                  pl.BlockSpec((k1, tn), lambda i, j: (0, j)),
                  pl.BlockSpec((k2, tn), lambda i, j: (k1 // k2, j))],
        out_specs=pl.BlockSpec((tm, tn), lambda i, j: (i, j)),
        out_shape=jax.ShapeDtypeStruct((tp + ts, n), F32),
        compiler_params=_cparams(("parallel", "parallel")),
        name="out_proj",
    )(yp_ssd, yp_gdn, ys_ssd, ys_gdn, w_out, w_out)


def _ssd_kernel(z_ref, xs_ref, bc_ref, sm_ref, wx_ref, wbc_ref, bx_ref, bbc_ref, dtb_ref, alog_ref,
                dexp_ref, nw_ref, cst_ref, st_ref, y_ref, cso_ref, sto_ref, extx, extbc, s_scr,
                *, q, nseq, nc):
    c = pl.program_id(1)
    nh = SSD_HEADS
    hg = SSD_HEADS // SSD_GROUPS
    wg = SSD_WIDTH // SSD_GROUPS
    p = SSD_HEAD_DIM
    n = SSD_STATE
    gn = SSD_GROUPS * n

    li = _iota((q, q), 0)
    si = _iota((q, q), 1)
    tril = si <= li
    tri = tril.astype(F32)
    expand = (_iota((nh, SSD_WIDTH), 1) // p == _iota((nh, SSD_WIDTH), 0)).astype(F32)
    expand_t = (_iota((SSD_WIDTH, nh), 0) // p == _iota((SSD_WIDTH, nh), 1)).astype(F32)
    lane = _iota((q, 128), 1)

    for s in range(nseq):
        rows = slice(s * q, (s + 1) * q)

        @pl.when(c == 0)
        def _():
            _conv_init(extx, s, cst_ref[s, :, 0:SSD_WIDTH])
            _conv_init(extbc, s, cst_ref[s, :, SSD_WIDTH:SSD_CONV_DIM])
            s_scr[s] = st_ref[s]

        xs = _conv_silu(xs_ref[rows, :], extx, s, wx_ref, bx_ref[...], q)
        bcv = _conv_silu(bc_ref[rows, :], extbc, s, wbc_ref, bbc_ref[...], q)

        @pl.when(c == nc - 1)
        def _():
            cso_ref[s, :, 0:SSD_WIDTH] = _conv_tail(extx, s, q)
            cso_ref[s, :, SSD_WIDTH:SSD_CONV_DIM] = _conv_tail(extbc, s, q)

        _conv_advance(extx, s, q)
        _conv_advance(extbc, s, q)

        dt = _softplus(sm_ref[rows, :][:, SMALL_DT:SMALL_DT + nh] + dtb_ref[...])
        a = dt * (-jnp.exp(alog_ref[...]))
        a_cum = _dot_exact_lhs(tri, a)
        a_cum_t = _transpose_cols(a_cum, nh)
        dt_e = _dot_exact_rhs(dt, expand)
        acum_e = _dot_exact_rhs(a_cum, expand)
        xdt = xs * dt_e
        alast_e = acum_e[q - 1:q, :]
        xd = xdt * jnp.exp(alast_e - acum_e)
        alast_b = jnp.broadcast_to(a_cum_t[:, q - 1:q], (nh, 128))
        dec_rows = jnp.exp(_dot_exact_lhs(expand_t, alast_b))

        pieces = []
        for g in range(SSD_GROUPS):
            bm = bcv[:, g * n:(g + 1) * n]
            cm = bcv[:, gn + g * n:gn + (g + 1) * n]
            cb = _dot(cm, bm, _NT)
            s_old = s_scr[s, g * wg:(g + 1) * wg, :]
            pieces.append(_dot(cm, s_old, _NT))
            s_scr[s, g * wg:(g + 1) * wg, :] = (dec_rows[g * wg:(g + 1) * wg, :] * s_old
                                                + _dot(xd[:, g * wg:(g + 1) * wg], bm, _TN))
            for k in range(hg // 2):
                ms = []
                for hh in (g * hg + 2 * k, g * hg + 2 * k + 1):
                    seg = a_cum[:, hh:hh + 1] - a_cum_t[hh:hh + 1, :]
                    ms.append(cb * jnp.exp(jnp.where(tril, seg, NEG_BIG)))
                c0 = g * wg + 2 * p * k
                xp = xdt[:, c0:c0 + 2 * p]
                top = jnp.where(lane < p, xp, 0.0)
                bot = jnp.where(lane >= p, xp, 0.0)
                if q % 128 == 0:
                    piece = _dot(jnp.concatenate(ms, axis=1), jnp.concatenate([top, bot], axis=0))
                else:
                    piece = _dot(ms[0], top) + _dot(ms[1], bot)
                pieces.append(piece)
        y_off = jnp.concatenate([pieces[0], pieces[1 + hg // 2]], axis=1) * jnp.exp(acum_e)
        y_diag = jnp.concatenate(pieces[1:1 + hg // 2] + pieces[2 + hg // 2:], axis=1)
        y = y_off + y_diag + dexp_ref[...] * xs
        y = y * _silu(z_ref[rows, :])
        outs = []
        for g in range(SSD_GROUPS):
            yg = y[:, g * wg:(g + 1) * wg]
            outs.append(yg * lax.rsqrt(jnp.mean(yg * yg, axis=-1, keepdims=True) + EPS))
        y_ref[rows, :] = (jnp.concatenate(outs, axis=1) * nw_ref[...]).astype(y_ref.dtype)

        @pl.when(c == nc - 1)
        def _():
            sto_ref[s] = s_scr[s]


def _ssd_call(proj, lw, conv_state, ssm_state, layer, *, row0, bsz, seq, q, nseq):
    nc = seq // q
    r = nseq * q
    assert nseq == 1 or nc == 1
    rb0 = row0 // r
    n = SSD_STATE
    bcw = SSD_CONV_DIM - SSD_WIDTH
    grid = (bsz // nseq, nc)
    rowi = lambda b, c: rb0 + b * nc + c
    full = lambda shape: pl.BlockSpec(shape, lambda b, c: (0,) * len(shape))

    in_specs = [
        pl.BlockSpec((r, SSD_WIDTH), lambda b, c: (rowi(b, c), COL_Z_SSD // SSD_WIDTH)),
        pl.BlockSpec((r, SSD_WIDTH), lambda b, c: (rowi(b, c), COL_X_SSD // SSD_WIDTH)),
        pl.BlockSpec((r, bcw), lambda b, c: (rowi(b, c), COL_BC // bcw)),
        pl.BlockSpec((r, 128), lambda b, c: (rowi(b, c), COL_SMALL // 128)),
        pl.BlockSpec((CONV_WIDTH, SSD_WIDTH), lambda b, c: (0, 0)),
        pl.BlockSpec((CONV_WIDTH, bcw), lambda b, c: (0, SSD_WIDTH // bcw)),
        pl.BlockSpec((1, SSD_WIDTH), lambda b, c: (0, 0)),
        pl.BlockSpec((1, bcw), lambda b, c: (0, SSD_WIDTH // bcw)),
        full((1, SSD_HEADS)), full((1, SSD_HEADS)), full((1, SSD_WIDTH)), full((1, SSD_WIDTH)),
        pl.BlockSpec((None, nseq, CONV_WIDTH - 1, SSD_CONV_DIM), lambda b, c: (layer, b, 0, 0)),
        pl.BlockSpec((None, nseq, SSD_WIDTH, n), lambda b, c: (layer, b, 0, 0)),
    ]
    cw = lw["w_conv_ssd"]
    cbias = lw["b_conv_ssd"].reshape(1, SSD_CONV_DIM)
    args = [proj, proj, proj, proj, cw, cw, cbias, cbias,
            lw["ssd_dt_bias"].reshape(1, SSD_HEADS), lw["ssd_a_log"].reshape(1, SSD_HEADS),
            jnp.repeat(lw["ssd_d"], SSD_HEAD_DIM).reshape(1, SSD_WIDTH), lw["ssd_norm"].reshape(1, SSD_WIDTH),
            conv_state, ssm_state]
    out_shape = [
        jax.ShapeDtypeStruct((bsz * seq, SSD_WIDTH), BF16),
        jax.ShapeDtypeStruct((bsz, CONV_WIDTH - 1, SSD_CONV_DIM), F32),
        jax.ShapeDtypeStruct((bsz, SSD_WIDTH, n), F32),
    ]
    out_specs = [
        pl.BlockSpec((r, SSD_WIDTH), lambda b, c: (b * nc + c, 0)),
        pl.BlockSpec((nseq, CONV_WIDTH - 1, SSD_CONV_DIM), lambda b, c: (b, 0, 0)),
        pl.BlockSpec((nseq, SSD_WIDTH, n), lambda b, c: (b, 0, 0)),
    ]
    scratch = [pltpu.VMEM((nseq, q + 8, SSD_WIDTH), F32), pltpu.VMEM((nseq, q + 8, bcw), F32),
               pltpu.VMEM((nseq, SSD_WIDTH, n), F32)]
    kern = functools.partial(_ssd_kernel, q=q, nseq=nseq, nc=nc)
    return pl.pallas_call(
        kern, grid=grid, in_specs=in_specs, out_specs=out_specs, out_shape=out_shape,
        scratch_shapes=scratch,
        compiler_params=_cparams(("parallel", "arbitrary")), name="ssd_mixer",
    )(*args)


def _neumann_inverse(xs, nil):
    n = xs[0].shape[0]
    eye = (_iota((n, n), 0) == _iota((n, n), 1)).astype(F32)
    invs = [eye - x for x in xs]
    pws = xs
    k = 2
    while k < nil:
        pws = [_dot(pw, pw) for pw in pws]
        invs = [inv + _dot(inv, pw) for inv, pw in zip(invs, pws)]
        k *= 2
    return invs


def _unit_lower_inverse(ms, q):
    blk = 16
    if q <= blk:
        return _neumann_inverse(ms, q)
    same = (_iota((q, q), 0) // blk) == (_iota((q, q), 1) // blk)
    dgs = [jnp.where(same, m, 0.0) for m in ms]
    dinvs = _neumann_inverse(dgs, blk)
    nns = [_dot(dinv, m - dg) for dinv, m, dg in zip(dinvs, ms, dgs)]
    ninvs = _neumann_inverse(nns, q // blk)
    return [_dot(ninv, dinv) for ninv, dinv in zip(ninvs, dinvs)]


def _gdn_kernel(qkv_ref, z_ref, sm_ref, w_ref, dtb_ref, alog_ref, nw_ref, cst_ref, st_ref,
                y_ref, cso_ref, sto_ref, ext, s_scr, *, q, nseq, nc):
    c = pl.program_id(1)
    nv = GDN_V_HEADS
    rep = GDN_V_HEADS // GDN_K_HEADS
    dk = GDN_HEAD
    dv = GDN_HEAD

    li = _iota((q, q), 0)
    si = _iota((q, q), 1)
    tril = si <= li
    strict = si < li
    tri = tril.astype(F32)

    heads = []
    for s in range(nseq):
        rows = slice(s * q, (s + 1) * q)

        @pl.when(c == 0)
        def _():
            _conv_init(ext, s, cst_ref[s])
            s_scr[s] = st_ref[s]

        xc = _conv_silu(qkv_ref[rows, :], ext, s, w_ref, None, q)

        @pl.when(c == nc - 1)
        def _():
            cso_ref[s] = _conv_tail(ext, s, q)

        _conv_advance(ext, s, q)

        sm = sm_ref[rows, :]
        a_raw = sm[:, SMALL_A:SMALL_A + nv]
        b_raw = sm[:, SMALL_B:SMALL_B + nv]
        gate = -jnp.exp(alog_ref[...]) * _softplus(a_raw + dtb_ref[...])
        g_cum = _dot_exact_lhs(tri, gate)
        g_cum_t = _transpose_cols(g_cum, nv)
        beta_all = _sigmoid(b_raw)

        for kh in range(GDN_K_HEADS):
            qc = xc[:, kh * dk:(kh + 1) * dk]
            kc = xc[:, GDN_QK_DIM + kh * dk:GDN_QK_DIM + (kh + 1) * dk]
            qn = qc * lax.rsqrt(jnp.sum(qc * qc, axis=-1, keepdims=True) + EPS) * (dk ** -0.5)
            kn = kc * lax.rsqrt(jnp.sum(kc * kc, axis=-1, keepdims=True) + EPS)
            kk = _dot(kn, kn, _NT)
            qk = _dot(qn, kn, _NT)
            for j in range(kh * rep, (kh + 1) * rep):
                col = g_cum[:, j:j + 1]
                seg = col - g_cum_t[j:j + 1, :]
                decay = jnp.exp(jnp.where(tril, seg, NEG_BIG))
                beta = beta_all[:, j:j + 1]
                e_col = jnp.exp(col)
                g_last = g_cum[q - 1:q, j:j + 1]
                vj = xc[:, 2 * GDN_QK_DIM + j * dv:2 * GDN_QK_DIM + (j + 1) * dv]
                heads.append(dict(
                    s=s, j=j, rows=rows,
                    m=jnp.where(strict, beta * kk * decay, 0.0),
                    rhs=jnp.concatenate([vj * beta, kn * (beta * e_col)], axis=1),
                    attn=qk * decay, q_dec=qn * e_col, k_dec=kn * jnp.exp(g_last - col),
                    s_dec=jnp.exp(g_last)))

    t_invs = _unit_lower_inverse([hd["m"] for hd in heads], q)
    sols = [_dot(t_inv, hd["rhs"]) for t_inv, hd in zip(t_invs, heads)]
    states = [s_scr[hd["s"], dk * hd["j"]:dk * (hd["j"] + 1), :] for hd in heads]
    boths = [_dot(jnp.concatenate([sol[:, dv:], hd["q_dec"]], axis=0), st)
             for sol, hd, st in zip(sols, heads, states)]
    v_news = [sol[:, :dv] - both[:q] for sol, both in zip(sols, boths)]
    outs = [both[q:] + _dot(hd["attn"], v_new) for both, hd, v_new in zip(boths, heads, v_news)]
    for hd, st, v_new in zip(heads, states, v_news):
        s_scr[hd["s"], dk * hd["j"]:dk * (hd["j"] + 1), :] = st * hd["s_dec"] + _dot(hd["k_dec"], v_new, _TN)
    for s in range(nseq):
        rows = slice(s * q, (s + 1) * q)
        ys = []
        for hd, o in zip(heads, outs):
            if hd["s"] == s:
                o = o * lax.rsqrt(jnp.mean(o * o, axis=-1, keepdims=True) + EPS) * nw_ref[...]
                ys.append(o * _silu(z_ref[rows, dv * hd["j"]:dv * (hd["j"] + 1)]))
        y_ref[rows, :] = jnp.concatenate(ys, axis=1).astype(y_ref.dtype)

        @pl.when(c == nc - 1)
        def _():
            sto_ref[s] = s_scr[s]


def _gdn_call(proj, lw, conv_state, gdn_state, layer, *, row0, bsz, seq, q, nseq):
    nc = seq // q
    r = nseq * q
    assert nseq == 1 or nc == 1
    rb0 = row0 // r
    hd = GDN_HEAD
    sw = GDN_V_HEADS * hd
    grid = (bsz // nseq, nc)
    rowi = lambda b, c: rb0 + b * nc + c
    full = lambda shape: pl.BlockSpec(shape, lambda b, c: (0,) * len(shape))

    in_specs = [
        pl.BlockSpec((r, GDN_CONV_DIM), lambda b, c: (rowi(b, c), COL_QKV // GDN_CONV_DIM)),
        pl.BlockSpec((r, GDN_WIDTH), lambda b, c: (rowi(b, c), COL_Z_GDN // GDN_WIDTH)),
        pl.BlockSpec((r, 128), lambda b, c: (rowi(b, c), COL_SMALL // 128)),
        full((CONV_WIDTH, GDN_CONV_DIM)), full((1, GDN_V_HEADS)), full((1, GDN_V_HEADS)), full((1, hd)),
        pl.BlockSpec((None, nseq, CONV_WIDTH - 1, GDN_CONV_DIM), lambda b, c: (layer, b, 0, 0)),
        pl.BlockSpec((None, nseq, sw, hd), lambda b, c: (layer, b, 0, 0)),
    ]
    args = [proj, proj, proj, lw["w_conv_gdn"], lw["gdn_dt_bias"].reshape(1, GDN_V_HEADS),
            lw["gdn_a_log"].reshape(1, GDN_V_HEADS), lw["gdn_norm"].reshape(1, hd), conv_state, gdn_state]
    out_shape = [
        jax.ShapeDtypeStruct((bsz * seq, GDN_WIDTH), BF16),
        jax.ShapeDtypeStruct((bsz, CONV_WIDTH - 1, GDN_CONV_DIM), F32),
        jax.ShapeDtypeStruct((bsz, sw, hd), F32),
    ]
    out_specs = [
        pl.BlockSpec((r, GDN_WIDTH), lambda b, c: (b * nc + c, 0)),
        pl.BlockSpec((nseq, CONV_WIDTH - 1, GDN_CONV_DIM), lambda b, c: (b, 0, 0)),
        pl.BlockSpec((nseq, sw, hd), lambda b, c: (b, 0, 0)),
    ]
    scratch = [pltpu.VMEM((nseq, q + 8, GDN_CONV_DIM), F32), pltpu.VMEM((nseq, sw, hd), F32)]
    kern = functools.partial(_gdn_kernel, q=q, nseq=nseq, nc=nc)
    return pl.pallas_call(
        kern, grid=grid, in_specs=in_specs, out_specs=out_specs, out_shape=out_shape,
        scratch_shapes=scratch,
        compiler_params=_cparams(("parallel", "arbitrary")), name="gdn_mixer",
    )(*args)


def _ffn_kernel(te_ref, xi_ref, nv_ref, x_ref, wg_ref, wu_ref, wd_ref, o_ref, acc_ref):
    i = pl.program_id(0)
    j = pl.program_id(1)
    valid = i < nv_ref[0]

    @pl.when(valid)
    def _():
        x = x_ref[...]
        gate = jnp.dot(x, wg_ref[0].astype(BF16), preferred_element_type=F32)
        up = jnp.dot(x, wu_ref[0].astype(BF16), preferred_element_type=F32)
        act = (_silu(gate) * up).astype(BF16)
        part = jnp.dot(act, wd_ref[0].astype(BF16), preferred_element_type=F32)

        @pl.when(j == 0)
        def _():
            acc_ref[...] = part

        @pl.when(j > 0)
        def _():
            acc_ref[...] += part

    last = j == pl.num_programs(1) - 1

    @pl.when(last & valid)
    def _():
        o_ref[...] = acc_ref[...]

    @pl.when(last & jnp.logical_not(valid))
    def _():
        o_ref[...] = jnp.zeros(o_ref.shape, o_ref.dtype)


def _ffn_call(x, w_gate, w_up, w_down, tile_expert, tile_src, n_valid):
    tp, d = x.shape
    f = w_gate.shape[2]
    tm, tf = FFN_TILE_M, FFN_TILE_F
    nf = f // tf
    nt = tp // tm

    def jeff(i, j, nv):
        return jnp.where(i < nv[0], j, nf - 1)

    grid_spec = pltpu.PrefetchScalarGridSpec(
        num_scalar_prefetch=3,
        grid=(nt, nf),
        in_specs=[
            pl.BlockSpec((tm, d), lambda i, j, te, xi, nv: (xi[i], 0)),
            pl.BlockSpec((1, d, tf), lambda i, j, te, xi, nv: (te[i], 0, jeff(i, j, nv))),
            pl.BlockSpec((1, d, tf), lambda i, j, te, xi, nv: (te[i], 0, jeff(i, j, nv))),
            pl.BlockSpec((1, tf, d), lambda i, j, te, xi, nv: (te[i], jeff(i, j, nv), 0)),
        ],
        out_specs=pl.BlockSpec((tm, d), lambda i, j, te, xi, nv: (i, 0)),
        scratch_shapes=[pltpu.VMEM((tm, d), F32)],
    )
    return pl.pallas_call(
        _ffn_kernel, grid_spec=grid_spec, out_shape=jax.ShapeDtypeStruct((tp, d), F32),
        compiler_params=_cparams(("arbitrary", "arbitrary")), name="grouped_swiglu",
    )(tile_expert, tile_src, n_valid, x, w_gate, w_up, w_down)


def _router_kernel(h_ref, wr_ref, sel_ref, cw_ref):
    logits = _dot6(wr_ref[...], h_ref[...], _NT)
    ne = logits.shape[0]
    ei = _iota(logits.shape, 0)
    m1 = jnp.max(logits, axis=0, keepdims=True)
    i1 = jnp.min(jnp.where(logits == m1, ei, ne), axis=0, keepdims=True)
    rest = jnp.where(ei == i1, -jnp.inf, logits)
    m2 = jnp.max(rest, axis=0, keepdims=True)
    i2 = jnp.min(jnp.where(rest == m2, ei, ne), axis=0, keepdims=True)
    e = jnp.exp(m2 - m1)
    w1 = 1.0 / (1.0 + e)
    w2 = e / (1.0 + e)
    sel_ref[...] = jnp.where(ei == i1, 1, jnp.where(ei == i2, 2, 0)).astype(jnp.int32)
    cw_ref[...] = jnp.where(ei == i1, w1, jnp.where(ei == i2, w2, 0.0))


def _router_call(h, w_router):
    t, d = h.shape
    ne = w_router.shape[1]
    tm = TOKEN_TILE
    return pl.pallas_call(
        _router_kernel, grid=(t // tm,),
        in_specs=[pl.BlockSpec((tm, d), lambda i: (i, 0)), pl.BlockSpec((ne, d), lambda i: (0, 0))],
        out_specs=[pl.BlockSpec((ne, tm), lambda i: (0, i)), pl.BlockSpec((ne, tm), lambda i: (0, i))],
        out_shape=[jax.ShapeDtypeStruct((ne, t), jnp.int32), jax.ShapeDtypeStruct((ne, t), F32)],
        compiler_params=_cparams(("parallel",)), name="router_top2",
    )(h, w_router.T)


def _gather_kernel(*refs, n_src, weighted, tm):
    nrows_ref = refs[0]
    idx_now = refs[1:1 + n_src]
    idx_next = refs[1 + n_src:1 + 2 * n_src]
    refs = refs[1 + 2 * n_src:]
    table = refs[0]
    refs = refs[1:]
    if weighted:
        w_ref = refs[0]
        refs = refs[1:]
    o_ref, buf, sem = refs
    i = pl.program_id(0)
    nrows = nrows_ref[0]

    def issue(idx_refs, slot):
        for k in range(n_src):
            def body(r8, carry, k=k):
                for u in range(GATHER_UNROLL):
                    r = r8 * GATHER_UNROLL + u
                    pltpu.make_async_copy(table.at[pl.ds(idx_refs[k][0, 0, r], 1), :],
                                          buf.at[slot, k, pl.ds(r, 1), :], sem.at[slot, k]).start()
                return carry
            lax.fori_loop(0, tm // GATHER_UNROLL, body, 0)

    slot = lax.rem(i, 2)
    valid = i * tm < nrows

    @pl.when((i == 0) & valid)
    def _():
        issue(idx_now, 0)

    @pl.when((i + 1) * tm < nrows)
    def _():
        issue(idx_next, 1 - slot)

    @pl.when(valid)
    def _():
        for k in range(n_src):
            pltpu.make_async_copy(table.at[pl.ds(0, tm), :], buf.at[slot, k], sem.at[slot, k]).wait()
        if weighted:
            acc = buf[slot, 0] * w_ref[:, 0:1]
            for k in range(1, n_src):
                acc = acc + buf[slot, k] * w_ref[:, k:k + 1]
            o_ref[...] = acc.astype(o_ref.dtype)
        else:
            o_ref[...] = buf[slot, 0].astype(o_ref.dtype)

    @pl.when(jnp.logical_not(valid))
    def _():
        o_ref[...] = jnp.zeros(o_ref.shape, o_ref.dtype)


def _gather_call(table, idx, weights, out_dtype, n_rows):
    n_src, t_out = idx.shape
    d = table.shape[1]
    tm = GATHER_TILE
    nt = t_out // tm
    idx3 = idx.reshape(n_src, nt, 1, tm)
    smem = functools.partial(pl.BlockSpec, memory_space=pltpu.SMEM)
    in_specs = ([smem((1, 1, tm), lambda i, nr: (i, 0, 0)) for _ in range(n_src)]
                + [smem((1, 1, tm), lambda i, nr: (jnp.minimum(i + 1, nt - 1), 0, 0)) for _ in range(n_src)]
                + [pl.BlockSpec(memory_space=pl.ANY)])
    args = [idx3[k] for k in range(n_src)] * 2 + [table]
    if weights is not None:
        in_specs.append(pl.BlockSpec((tm, n_src), lambda i, nr: (i, 0)))
        args.append(weights)
    kern = functools.partial(_gather_kernel, n_src=n_src, weighted=weights is not None, tm=tm)
    grid_spec = pltpu.PrefetchScalarGridSpec(
        num_scalar_prefetch=1, grid=(nt,), in_specs=in_specs,
        out_specs=pl.BlockSpec((tm, d), lambda i, nr: (i, 0)),
        scratch_shapes=[pltpu.VMEM((2, n_src, tm, d), table.dtype), pltpu.SemaphoreType.DMA((2, n_src))],
    )
    return pl.pallas_call(
        kern, grid_spec=grid_spec, out_shape=jax.ShapeDtypeStruct((t_out, d), out_dtype),
        compiler_params=_cparams(("arbitrary",)), name="row_gather",
    )(jnp.asarray(n_rows, jnp.int32).reshape(1), *args)


def _moe(h, w_router, w_gate, w_up, w_down):
    t, d = h.shape
    ne = w_router.shape[1]
    tm = FFN_TILE_M
    sel, cw = _router_call(h, w_router)
    hot = (sel > 0).astype(jnp.int32)
    rank = jnp.cumsum(hot, axis=1) - hot
    counts = jnp.sum(hot, axis=1)
    tiles_e = (counts + tm - 1) // tm
    tile_end = jnp.cumsum(tiles_e)
    tile_start = tile_end - tiles_e
    n_valid = tile_end[-1]
    n_tiles = (2 * t + tm - 1) // tm + ne
    tp = n_tiles * tm
    dest = tile_start[:, None] * tm + rank
    tok = jnp.broadcast_to(jnp.arange(t, dtype=jnp.int32)[None, :], (ne, t))
    src = jnp.zeros((tp + 1,), jnp.int32).at[jnp.where(hot > 0, dest, tp).reshape(-1)].set(tok.reshape(-1))[:tp]
    tid = jnp.arange(n_tiles, dtype=jnp.int32)
    te = jnp.sum((tid[:, None] >= tile_end[None, :]).astype(jnp.int32), axis=1)
    last = jnp.maximum(n_valid - 1, 0)
    te = jnp.where(tid < n_valid, te, te[last]).astype(jnp.int32)
    xi = jnp.minimum(tid, last).astype(jnp.int32)
    p1 = jnp.sum(jnp.where(sel == 1, dest, 0), axis=0)
    p2 = jnp.sum(jnp.where(sel == 2, dest, 0), axis=0)
    wts = jnp.stack([jnp.sum(jnp.where(sel == 1, cw, 0.0), axis=0),
                     jnp.sum(jnp.where(sel == 2, cw, 0.0), axis=0)], axis=1)

    xs = _gather_call(h, src[None, :], None, BF16, n_valid * tm)
    ys = _ffn_call(xs, w_gate, w_up, w_down, te, xi, n_valid.reshape(1).astype(jnp.int32))
    return _gather_call(ys, jnp.stack([p1, p2]).astype(jnp.int32), wts, F32, t)


def _reorder_in_proj(w_in):
    o = 0
    seg = {}
    for name, width in (("z_ssd", SSD_WIDTH), ("xs", SSD_WIDTH), ("bc", SSD_CONV_DIM - SSD_WIDTH),
                        ("dt", SSD_HEADS), ("qkv", GDN_CONV_DIM), ("z_gdn", GDN_WIDTH),
                        ("a", GDN_V_HEADS), ("b", GDN_V_HEADS)):
        seg[name] = w_in[:, :, o:o + width]
        o += width
    pad = jnp.zeros(w_in.shape[:2] + (128 - SSD_HEADS - 2 * GDN_V_HEADS,), w_in.dtype)
    parts = [seg["qkv"], seg["z_ssd"], seg["z_gdn"], seg["xs"], seg["bc"], seg["dt"], seg["a"], seg["b"], pad]
    return jnp.concatenate(parts, axis=-1).astype(BF16)


def kernel(x_prompt, x_sample, c_prompt, c_sample, state_ssd_conv, state_ssm, state_gdn_conv, state_gdn, w_ada, b_ada, w_norm_mix, w_norm_ffn, w_in, w_conv_ssd, b_conv_ssd, ssd_dt_bias, ssd_a_log, ssd_d, ssd_norm, w_conv_gdn, gdn_dt_bias, gdn_a_log, gdn_norm, w_out, w_ffn_gate, w_ffn_up, w_ffn_down, w_router, w_exp_gate, w_exp_up, w_exp_down, w_norm_final):
    bp, lp, d = x_prompt.shape
    bs, ls, _ = x_sample.shape
    depth = w_in.shape[0]
    tpr = bp * lp
    tsa = bs * ls
    t = tpr + tsa

    x = jnp.concatenate([x_prompt.reshape(tpr, d), x_sample.reshape(tsa, d)], axis=0)
    mod = _ada_call(jnp.concatenate([c_sample, c_prompt], axis=0), w_ada, b_ada)
    w_in_r = _reorder_in_proj(w_in)
    w_out_b = w_out.astype(BF16)

    qp = PROMPT_CHUNK if lp % PROMPT_CHUNK == 0 else lp
    nseq_s = 16 // ls if ls < 16 else 1
    sw = GDN_V_HEADS * GDN_HEAD
    zeros_p = (jnp.zeros((1, bp, CONV_WIDTH - 1, SSD_CONV_DIM), F32), jnp.zeros((1, bp, SSD_WIDTH, SSD_STATE), F32),
               jnp.zeros((1, bp, CONV_WIDTH - 1, GDN_CONV_DIM), F32), jnp.zeros((1, bp, sw, GDN_HEAD), F32))
    ssm_in = state_ssm.reshape(depth, bs, SSD_WIDTH, SSD_STATE)
    gdn_in = state_gdn.reshape(depth, bs, sw, GDN_HEAD)

    new_p = [[], [], [], []]
    new_s = [[], [], [], []]
    common = dict(t_prompt=tpr, l_prompt=lp)
    mods = [(mod[l, bs:].reshape(bp, 1, N_MOD * d), jnp.repeat(mod[l, :bs], ls, axis=0)) for l in range(depth)]
    h = _ew_call(x, w_norm_mix[0], mod=(1, 0) + mods[0], **common)
    for l in range(depth):
        mod_p, mod_s = mods[l]
        lw = {"w_conv_ssd": w_conv_ssd[l], "b_conv_ssd": b_conv_ssd[l], "ssd_dt_bias": ssd_dt_bias[l],
              "ssd_a_log": ssd_a_log[l], "ssd_d": ssd_d[l], "ssd_norm": ssd_norm[l],
              "w_conv_gdn": w_conv_gdn[l], "gdn_dt_bias": gdn_dt_bias[l], "gdn_a_log": gdn_a_log[l],
              "gdn_norm": gdn_norm[l]}
        proj = _mm_call(h, w_in_r[l], 640)

        prompt = dict(row0=0, bsz=bp, seq=lp, q=qp, nseq=1)
        sample = dict(row0=tpr, bsz=bs, seq=ls, q=ls, nseq=nseq_s)
        yp_ssd, pc, pst = _ssd_call(proj, lw, zeros_p[0], zeros_p[1], 0, **prompt)
        ys_ssd, sc_, sst = _ssd_call(proj, lw, state_ssd_conv, ssm_in, l, **sample)
        yp_gdn, pgc, pgst = _gdn_call(proj, lw, zeros_p[2], zeros_p[3], 0, **prompt)
        ys_gdn, sgc, sgst = _gdn_call(proj, lw, state_gdn_conv, gdn_in, l, **sample)
        for lst, v in zip(new_p, (pc, pst, pgc, pgst)):
            lst.append(v)
        for lst, v in zip(new_s, (sc_, sst, sgc, sgst)):
            lst.append(v)

        mix = _out_proj_call(yp_ssd, yp_gdn, ys_ssd, ys_gdn, w_out_b[l])
        moe_layer = l % 2 == 1
        x, h2 = _ew_call(x, w_norm_ffn[l], res=(mix, 2, mod_p, mod_s), mod=(4, 3, mod_p, mod_s), out_x=True,
                         h_dtype=F32 if moe_layer else BF16, **common)
        i = l // 2
        if moe_layer:
            f = _moe(h2, w_router[i], w_exp_gate[i], w_exp_up[i], w_exp_down[i])
        else:
            nt = t // FFN_TILE_M
            tid = jnp.arange(nt, dtype=jnp.int32)
            f = _ffn_call(h2, w_ffn_gate[i][None], w_ffn_up[i][None], w_ffn_down[i][None],
                          jnp.zeros((nt,), jnp.int32), tid, jnp.full((1,), nt, jnp.int32))
        if l + 1 < depth:
            x, h = _ew_call(x, w_norm_mix[l + 1], res=(f, 5, mod_p, mod_s), mod=(1, 0) + mods[l + 1],
                            out_x=True, **common)
        else:
            y = _ew_call(x, w_norm_final, res=(f, 5, mod_p, mod_s), h_dtype=F32, **common)

    def stack_states(lst, bsz):
        conv_s = jnp.stack(lst[0])
        ssm = jnp.stack(lst[1]).reshape(depth, bsz, SSD_HEADS, SSD_HEAD_DIM, SSD_STATE)
        conv_g = jnp.stack(lst[2])
        gdn = jnp.stack(lst[3]).reshape(depth, bsz, GDN_V_HEADS, GDN_HEAD, GDN_HEAD)
        return conv_s, ssm, conv_g, gdn

    return ((y[:tpr].reshape(bp, lp, d), y[tpr:].reshape(bs, ls, d))
            + stack_states(new_p, bp) + stack_states(new_s, bs))
```

```python
import functools

import jax
import jax.numpy as jnp
from jax import lax
from jax.experimental import pallas as pl
from jax.experimental.pallas import tpu as pltpu

F32 = jnp.float32
BF16 = jnp.bfloat16

D_MODEL = 2048
CONV_WIDTH = 4
SSD_WIDTH = 1024
SSD_HEAD_DIM = 64
SSD_HEADS = 16
SSD_GROUPS = 2
SSD_STATE = 128
SSD_CONV_DIM = SSD_WIDTH + 2 * SSD_GROUPS * SSD_STATE
GDN_WIDTH = 1024
GDN_HEAD = 128
GDN_V_HEADS = 8
GDN_K_HEADS = 4
GDN_QK_DIM = GDN_K_HEADS * GDN_HEAD
GDN_CONV_DIM = 2 * GDN_QK_DIM + GDN_WIDTH
N_EXPERTS = 8
N_MOD = 6
EPS = 1e-6

PROJ_COLS = 5760
COL_QKV, COL_Z_SSD, COL_Z_GDN, COL_X_SSD, COL_BC, COL_SMALL = 0, 2048, 3072, 4096, 5120, 5632
SMALL_DT, SMALL_A, SMALL_B = 0, SSD_HEADS, SSD_HEADS + GDN_V_HEADS

V7X_VMEM_LIMIT = 56 * 1024 * 1024
TOKEN_TILE = 256
MM_TILE_M = 1024
IN_PROJ_TILE_N = 1920
FFN_TILE_M = 768
FFN_TILE_F = 256
GATHER_TILE = 256
GATHER_UNROLL = 8
PROMPT_CHUNK = 128
NEG_BIG = -1e30


def _cparams(sem):
    return pltpu.CompilerParams(dimension_semantics=sem, vmem_limit_bytes=V7X_VMEM_LIMIT)


_NN = (((1,), (0,)), ((), ()))
_NT = (((1,), (1,)), ((), ()))
_TN = (((0,), (0,)), ((), ()))


def _dg(a, b, dims):
    return lax.dot_general(a, b, dims, preferred_element_type=F32)


def _dot(a, b, dims=_NN):
    return _dg(a.astype(BF16), b.astype(BF16), dims)


def _split(x, n):
    parts = []
    r = x
    for _ in range(n - 1):
        p = r.astype(BF16)
        parts.append(p)
        r = r - p.astype(F32)
    parts.append(r.astype(BF16))
    return parts


def _dot_exact_rhs(a, b, dims=_NN):
    bb = b.astype(BF16)
    a1, a2, a3 = _split(a, 3)
    return _dg(a3, bb, dims) + _dg(a2, bb, dims) + _dg(a1, bb, dims)


def _dot_exact_lhs(a, b, dims=_NN):
    ab = a.astype(BF16)
    b1, b2, b3 = _split(b, 3)
    return _dg(ab, b3, dims) + _dg(ab, b2, dims) + _dg(ab, b1, dims)


def _dot6(a, b, dims=_NN):
    a1, a2, a3 = _split(a, 3)
    b1, b2, b3 = _split(b, 3)
    small = _dg(a1, b3, dims) + _dg(a2, b2, dims) + _dg(a3, b1, dims)
    mid = _dg(a1, b2, dims) + _dg(a2, b1, dims)
    return small + mid + _dg(a1, b1, dims)


def _sigmoid(x):
    return 1.0 / (1.0 + jnp.exp(-x))


def _silu(x):
    return x * _sigmoid(x)


def _softplus(x):
    return jnp.maximum(x, 0.0) + jnp.log1p(jnp.exp(-jnp.abs(x)))


def _iota(shape, dim):
    return lax.broadcasted_iota(jnp.int32, shape, dim)


def _transpose_cols(x, n):
    eye = (_iota((n, n), 0) == _iota((n, n), 1)).astype(F32)
    return _dot_exact_lhs(eye, x, _NT)


def _conv_silu(u, ext_ref, s, w_ref, bias, q):
    ext_ref[s, 8:8 + q, :] = u
    acc = u * w_ref[CONV_WIDTH - 1:CONV_WIDTH, :]
    if bias is not None:
        acc = acc + bias
    for j in range(1, CONV_WIDTH):
        acc = acc + ext_ref[s, 8 - j:8 - j + q, :] * w_ref[CONV_WIDTH - 1 - j:CONV_WIDTH - j, :]
    return _silu(acc)


def _conv_advance(ext_ref, s, q):
    ext_ref[s, 0:8, :] = ext_ref[s, q:q + 8, :]


def _conv_init(ext_ref, s, state_rows):
    ext_ref[s, 0:8, :] = jnp.zeros((8, ext_ref.shape[-1]), F32)
    ext_ref[s, 8 - (CONV_WIDTH - 1):8, :] = state_rows


def _conv_tail(ext_ref, s, q):
    return ext_ref[s, q + 8 - (CONV_WIDTH - 1):q + 8, :]


def _ada_kernel(c_ref, w_ref, b_ref, o_ref):
    a = _silu(c_ref[...]).astype(BF16)
    o_ref[0] = jnp.dot(a, w_ref[0].astype(BF16), preferred_element_type=F32) + b_ref[0]


def _ada_call(c_all, w_ada, b_ada):
    depth, d, n = w_ada.shape
    m = c_all.shape[0]
    tn = 1024
    return pl.pallas_call(
        _ada_kernel,
        grid=(depth, n // tn),
        in_specs=[
            pl.BlockSpec((m, d), lambda l, j: (0, 0)),
            pl.BlockSpec((1, d, tn), lambda l, j: (l, 0, j)),
            pl.BlockSpec((1, 1, tn), lambda l, j: (l, 0, j)),
        ],
        out_specs=pl.BlockSpec((1, m, tn), lambda l, j: (l, 0, j)),
        out_shape=jax.ShapeDtypeStruct((depth, m, n), F32),
        compiler_params=_cparams(("parallel", "parallel")),
        name="ada_mod",
    )(c_all, w_ada, b_ada.reshape(depth, 1, n))


def _ew_kernel(*refs, split_in, has_res, modded, out_x, split_out, n_prompt_tiles):
    refs = list(refs)
    xp_ref = refs.pop(0)
    xs_ref = refs.pop(0) if split_in else xp_ref
    if has_res:
        y_ref, gp_ref, gs_ref = refs.pop(0), refs.pop(0), refs.pop(0)
    w_ref = refs.pop(0)
    if modded:
        scp_ref, shp_ref, scs_ref, shs_ref = refs.pop(0), refs.pop(0), refs.pop(0), refs.pop(0)
    if out_x:
        xo_ref = refs.pop(0)
    hp_ref = refs.pop(0)
    hs_ref = refs.pop(0) if split_out else hp_ref
    i = pl.program_id(0)

    def run(prompt):
        x = (xp_ref if prompt else xs_ref)[...]
        if has_res:
            g = gp_ref[0] if prompt else gs_ref[...]
            x = x + g * y_ref[...]
            if out_x:
                xo_ref[...] = x
        h = x * lax.rsqrt(jnp.mean(x * x, axis=-1, keepdims=True) + EPS) * w_ref[...]
        if modded:
            sc = scp_ref[0] if prompt else scs_ref[...]
            sh = shp_ref[0] if prompt else shs_ref[...]
            h = h * (1.0 + sc) + sh
        h_ref = hp_ref if prompt else hs_ref
        h_ref[...] = h.astype(h_ref.dtype)

    @pl.when(i < n_prompt_tiles)
    def _():
        run(True)

    @pl.when(i >= n_prompt_tiles)
    def _():
        run(False)


def _ew_call(x, w, *, t_prompt, l_prompt, res=None, mod=None, out_x=False, h_dtype=BF16, split_out=False):
    split_in = isinstance(x, tuple)
    d = w.shape[0]
    t = sum(a.shape[0] for a in x) if split_in else x.shape[0]
    tm = TOKEN_TILE
    npt = t_prompt // tm
    per_seq = l_prompt // tm

    def pspec(chunk, mp):
        bp = mp.shape[0]
        return pl.BlockSpec((1, 1, d), lambda i: (jnp.minimum(i // per_seq, bp - 1), 0, chunk))

    def sspec(chunk):
        return pl.BlockSpec((tm, d), lambda i: (jnp.maximum(i - npt, 0), chunk))

    row = pl.BlockSpec((tm, d), lambda i: (i, 0))
    prow = pl.BlockSpec((tm, d), lambda i: (jnp.minimum(i, npt - 1), 0))
    srow = sspec(0)
    args, specs = (list(x), [prow, srow]) if split_in else ([x], [row])
    if res is not None:
        y, gch, gmp, gms = res
        args += [y, gmp, gms]
        specs += [row, pspec(gch, gmp), sspec(gch)]
    args.append(w.reshape(1, d))
    specs.append(pl.BlockSpec((1, d), lambda i: (0, 0)))
    if mod is not None:
        scc, shc, mmp, mms = mod
        args += [mmp, mmp, mms, mms]
        specs += [pspec(scc, mmp), pspec(shc, mmp), sspec(scc), sspec(shc)]
    out_shape, out_specs = [], []
    if out_x:
        out_shape.append(jax.ShapeDtypeStruct((t, d), F32))
        out_specs.append(row)
    if split_out:
        out_shape += [jax.ShapeDtypeStruct((t_prompt, d), h_dtype), jax.ShapeDtypeStruct((t - t_prompt, d), h_dtype)]
        out_specs += [prow, srow]
    else:
        out_shape.append(jax.ShapeDtypeStruct((t, d), h_dtype))
        out_specs.append(row)
    kern = functools.partial(_ew_kernel, split_in=split_in, has_res=res is not None, modded=mod is not None,
                             out_x=out_x, split_out=split_out, n_prompt_tiles=npt)
    outs = pl.pallas_call(
        kern, grid=(t // tm,), in_specs=specs, out_specs=out_specs, out_shape=out_shape,
        compiler_params=_cparams(("arbitrary",)), name="token_norm",
    )(*args)
    return outs if len(outs) > 1 else outs[0]


def _mm_kernel(a_ref, w_ref, o_ref):
    o_ref[...] = jnp.dot(a_ref[...], w_ref[...].astype(BF16), preferred_element_type=F32)


def _mm_call(a, w, tn):
    m, k = a.shape
    n = w.shape[1]
    tm = MM_TILE_M
    while m % tm:
        tm //= 2
    return pl.pallas_call(
        _mm_kernel,
        grid=(m // tm, n // tn),
        in_specs=[pl.BlockSpec((tm, k), lambda i, j: (i, 0)), pl.BlockSpec((k, tn), lambda i, j: (0, j))],
        out_specs=pl.BlockSpec((tm, tn), lambda i, j: (i, j)),
        out_shape=jax.ShapeDtypeStruct((m, n), F32),
        compiler_params=_cparams(("parallel", "parallel")),
        name="in_proj",
    )(a, w)


def _out_proj_kernel(ap1_ref, ap2_ref, as1_ref, as2_ref, w1_ref, w2_ref, o_ref, *, n_prompt_tiles):
    i = pl.program_id(0)

    def run(a1_ref, a2_ref):
        o_ref[...] = (jnp.dot(a1_ref[...], w1_ref[...], preferred_element_type=F32)
                      + jnp.dot(a2_ref[...], w2_ref[...], preferred_element_type=F32))

    @pl.when(i < n_prompt_tiles)
    def _():
        run(ap1_ref, ap2_ref)

    @pl.when(i >= n_prompt_tiles)
    def _():
        run(as1_ref, as2_ref)


def _out_proj_call(yp_ssd, yp_gdn, ys_ssd, ys_gdn, w_out):
    tp, k1 = yp_ssd.shape
    ts = ys_ssd.shape[0]
    k2 = yp_gdn.shape[1]
    n = w_out.shape[1]
    tm = MM_TILE_M
    while tp % tm or ts % tm:
        tm //= 2
    tn = 1024
    npt = tp // tm
    pmap = lambda i, j: (jnp.minimum(i, npt - 1), 0)
    smap = lambda i, j: (jnp.maximum(i - npt, 0), 0)
    return pl.pallas_call(
        functools.partial(_out_proj_kernel, n_prompt_tiles=npt),
        grid=((tp + ts) // tm, n // tn),
        in_specs=[pl.BlockSpec((tm, k1), pmap), pl.BlockSpec((tm, k2), pmap),
                  pl.BlockSpec((tm, k1), smap), pl.BlockSpec((tm, k2), smap),
                  pl.BlockSpec((k1, tn), lambda i, j: (0, j)),
                  pl.BlockSpec((k2, tn), lambda i, j: (k1 // k2, j))],
        out_specs=pl.BlockSpec((tm, tn), lambda i, j: (i, j)),
        out_shape=jax.ShapeDtypeStruct((tp + ts, n), F32),
        compiler_params=_cparams(("parallel", "parallel")),
        name="out_proj",
    )(yp_ssd, yp_gdn, ys_ssd, ys_gdn, w_out, w_out)


def _ssd_kernel(z_ref, xs_ref, bc_ref, sm_ref, wx_ref, wbc_ref, bx_ref, bbc_ref, dtb_ref, alog_ref,
                dexp_ref, nw_ref, cst_ref, st_ref, y_ref, cso_ref, sto_ref, extx, extbc, s_scr,
                *, q, nseq, nc):
    c = pl.program_id(1)
    nh = SSD_HEADS
    hg = SSD_HEADS // SSD_GROUPS
    wg = SSD_WIDTH // SSD_GROUPS
    p = SSD_HEAD_DIM
    n = SSD_STATE
    gn = SSD_GROUPS * n

    li = _iota((q, q), 0)
    si = _iota((q, q), 1)
    tril = si <= li
    tri = tril.astype(F32)
    expand = (_iota((nh, SSD_WIDTH), 1) // p == _iota((nh, SSD_WIDTH), 0)).astype(F32)
    expand_t = (_iota((SSD_WIDTH, nh), 0) // p == _iota((SSD_WIDTH, nh), 1)).astype(F32)
    lane = _iota((q, 128), 1)

    for s in range(nseq):
        rows = slice(s * q, (s + 1) * q)

        @pl.when(c == 0)
        def _():
            _conv_init(extx, s, cst_ref[s, :, 0:SSD_WIDTH])
            _conv_init(extbc, s, cst_ref[s, :, SSD_WIDTH:SSD_CONV_DIM])
            s_scr[s] = st_ref[s]

        xs = _conv_silu(xs_ref[rows, :], extx, s, wx_ref, bx_ref[...], q)
        bcv = _conv_silu(bc_ref[rows, :], extbc, s, wbc_ref, bbc_ref[...], q)

        @pl.when(c == nc - 1)
        def _():
            cso_ref[s, :, 0:SSD_WIDTH] = _conv_tail(extx, s, q)
            cso_ref[s, :, SSD_WIDTH:SSD_CONV_DIM] = _conv_tail(extbc, s, q)

        _conv_advance(extx, s, q)
        _conv_advance(extbc, s, q)

        dt = _softplus(sm_ref[rows, :][:, SMALL_DT:SMALL_DT + nh] + dtb_ref[...])
        a = dt * (-jnp.exp(alog_ref[...]))
        a_cum = _dot_exact_lhs(tri, a)
        a_cum_t = _transpose_cols(a_cum, nh)
        dt_e = _dot_exact_rhs(dt, expand)
        acum_e = _dot_exact_rhs(a_cum, expand)
        xdt = xs * dt_e
        alast_e = acum_e[q - 1:q, :]
        xd = xdt * jnp.exp(alast_e - acum_e)
        alast_b = jnp.broadcast_to(a_cum_t[:, q - 1:q], (nh, 128))
        dec_rows = jnp.exp(_dot_exact_lhs(expand_t, alast_b))

        pieces = []
        for g in range(SSD_GROUPS):
            bm = bcv[:, g * n:(g + 1) * n]
            cm = bcv[:, gn + g * n:gn + (g + 1) * n]
            cb = _dot(cm, bm, _NT)
            s_old = s_scr[s, g * wg:(g + 1) * wg, :]
            pieces.append(_dot(cm, s_old, _NT))
            s_scr[s, g * wg:(g + 1) * wg, :] = (dec_rows[g * wg:(g + 1) * wg, :] * s_old
                                                + _dot(xd[:, g * wg:(g + 1) * wg], bm, _TN))
            for k in range(hg // 2):
                ms = []
                for hh in (g * hg + 2 * k, g * hg + 2 * k + 1):
                    seg = a_cum[:, hh:hh + 1] - a_cum_t[hh:hh + 1, :]
                    ms.append(cb * jnp.exp(jnp.where(tril, seg, NEG_BIG)))
                c0 = g * wg + 2 * p * k
                xp = xdt[:, c0:c0 + 2 * p]
                top = jnp.where(lane < p, xp, 0.0)
                bot = jnp.where(lane >= p, xp, 0.0)
                if q % 128 == 0:
                    piece = _dot(jnp.concatenate(ms, axis=1), jnp.concatenate([top, bot], axis=0))
                else:
                    piece = _dot(ms[0], top) + _dot(ms[1], bot)
                pieces.append(piece)
        y_off = jnp.concatenate([pieces[0], pieces[1 + hg // 2]], axis=1) * jnp.exp(acum_e)
        y_diag = jnp.concatenate(pieces[1:1 + hg // 2] + pieces[2 + hg // 2:], axis=1)
        y = y_off + y_diag + dexp_ref[...] * xs
        y = y * _silu(z_ref[rows, :])
        outs = []
        for g in range(SSD_GROUPS):
            yg = y[:, g * wg:(g + 1) * wg]
            outs.append(yg * lax.rsqrt(jnp.mean(yg * yg, axis=-1, keepdims=True) + EPS))
        y_ref[rows, :] = (jnp.concatenate(outs, axis=1) * nw_ref[...]).astype(y_ref.dtype)

        @pl.when(c == nc - 1)
        def _():
            sto_ref[s] = s_scr[s]


def _ssd_call(proj, lw, conv_state, ssm_state, layer, *, row0, bsz, seq, q, nseq):
    nc = seq // q
    r = nseq * q
    assert nseq == 1 or nc == 1
    rb0 = row0 // r
    n = SSD_STATE
    bcw = SSD_CONV_DIM - SSD_WIDTH
    grid = (bsz // nseq, nc)
    rowi = lambda b, c: rb0 + b * nc + c
    full = lambda shape: pl.BlockSpec(shape, lambda b, c: (0,) * len(shape))

    in_specs = [
        pl.BlockSpec((r, SSD_WIDTH), lambda b, c: (rowi(b, c), COL_Z_SSD // SSD_WIDTH)),
        pl.BlockSpec((r, SSD_WIDTH), lambda b, c: (rowi(b, c), COL_X_SSD // SSD_WIDTH)),
        pl.BlockSpec((r, bcw), lambda b, c: (rowi(b, c), COL_BC // bcw)),
        pl.BlockSpec((r, 128), lambda b, c: (rowi(b, c), COL_SMALL // 128)),
        pl.BlockSpec((CONV_WIDTH, SSD_WIDTH), lambda b, c: (0, 0)),
        pl.BlockSpec((CONV_WIDTH, bcw), lambda b, c: (0, SSD_WIDTH // bcw)),
        pl.BlockSpec((1, SSD_WIDTH), lambda b, c: (0, 0)),
        pl.BlockSpec((1, bcw), lambda b, c: (0, SSD_WIDTH // bcw)),
        full((1, SSD_HEADS)), full((1, SSD_HEADS)), full((1, SSD_WIDTH)), full((1, SSD_WIDTH)),
        pl.BlockSpec((None, nseq, CONV_WIDTH - 1, SSD_CONV_DIM), lambda b, c: (layer, b, 0, 0)),
        pl.BlockSpec((None, nseq, SSD_WIDTH, n), lambda b, c: (layer, b, 0, 0)),
    ]
    cw = lw["w_conv_ssd"]
    cbias = lw["b_conv_ssd"].reshape(1, SSD_CONV_DIM)
    args = [proj, proj, proj, proj, cw, cw, cbias, cbias,
            lw["ssd_dt_bias"].reshape(1, SSD_HEADS), lw["ssd_a_log"].reshape(1, SSD_HEADS),
            jnp.repeat(lw["ssd_d"], SSD_HEAD_DIM).reshape(1, SSD_WIDTH), lw["ssd_norm"].reshape(1, SSD_WIDTH),
            conv_state, ssm_state]
    out_shape = [
        jax.ShapeDtypeStruct((bsz * seq, SSD_WIDTH), BF16),
        jax.ShapeDtypeStruct((bsz, CONV_WIDTH - 1, SSD_CONV_DIM), F32),
        jax.ShapeDtypeStruct((bsz, SSD_WIDTH, n), F32),
    ]
    out_specs = [
        pl.BlockSpec((r, SSD_WIDTH), lambda b, c: (b * nc + c, 0)),
        pl.BlockSpec((nseq, CONV_WIDTH - 1, SSD_CONV_DIM), lambda b, c: (b, 0, 0)),
        pl.BlockSpec((nseq, SSD_WIDTH, n), lambda b, c: (b, 0, 0)),
    ]
    scratch = [pltpu.VMEM((nseq, q + 8, SSD_WIDTH), F32), pltpu.VMEM((nseq, q + 8, bcw), F32),
               pltpu.VMEM((nseq, SSD_WIDTH, n), F32)]
    kern = functools.partial(_ssd_kernel, q=q, nseq=nseq, nc=nc)
    return pl.pallas_call(
        kern, grid=grid, in_specs=in_specs, out_specs=out_specs, out_shape=out_shape,
        scratch_shapes=scratch,
        compiler_params=_cparams(("parallel", "arbitrary")), name="ssd_mixer",
    )(*args)


def _neumann_inverse(xs, nil):
    n = xs[0].shape[0]
    eye = (_iota((n, n), 0) == _iota((n, n), 1)).astype(F32)
    invs = [eye - x for x in xs]
    pws = xs
    k = 2
    while k < nil:
        pws = [_dot(pw, pw) for pw in pws]
        invs = [inv + _dot(inv, pw) for inv, pw in zip(invs, pws)]
        k *= 2
    return invs


def _unit_lower_inverse(ms, q):
    blk = 16
    if q <= blk:
        return _neumann_inverse(ms, q)
    same = (_iota((q, q), 0) // blk) == (_iota((q, q), 1) // blk)
    dgs = [jnp.where(same, m, 0.0) for m in ms]
    dinvs = _neumann_inverse(dgs, blk)
    nns = [_dot(dinv, m - dg) for dinv, m, dg in zip(dinvs, ms, dgs)]
    ninvs = _neumann_inverse(nns, q // blk)
    return [_dot(ninv, dinv) for ninv, dinv in zip(ninvs, dinvs)]


def _gdn_kernel(qkv_ref, z_ref, sm_ref, w_ref, dtb_ref, alog_ref, nw_ref, cst_ref, st_ref,
                y_ref, cso_ref, sto_ref, ext, s_scr, *, q, nseq, nc):
    c = pl.program_id(1)
    nv = GDN_V_HEADS
    rep = GDN_V_HEADS // GDN_K_HEADS
    dk = GDN_HEAD
    dv = GDN_HEAD

    li = _iota((q, q), 0)
    si = _iota((q, q), 1)
    tril = si <= li
    strict = si < li
    tri = tril.astype(F32)

    heads = []
    for s in range(nseq):
        rows = slice(s * q, (s + 1) * q)

        @pl.when(c == 0)
        def _():
            _conv_init(ext, s, cst_ref[s])
            s_scr[s] = st_ref[s]

        xc = _conv_silu(qkv_ref[rows, :], ext, s, w_ref, None, q)

        @pl.when(c == nc - 1)
        def _():
            cso_ref[s] = _conv_tail(ext, s, q)

        _conv_advance(ext, s, q)

        sm = sm_ref[rows, :]
        a_raw = sm[:, SMALL_A:SMALL_A + nv]
        b_raw = sm[:, SMALL_B:SMALL_B + nv]
        gate = -jnp.exp(alog_ref[...]) * _softplus(a_raw + dtb_ref[...])
        g_cum = _dot_exact_lhs(tri, gate)
        g_cum_t = _transpose_cols(g_cum, nv)
        beta_all = _sigmoid(b_raw)

        for kh in range(GDN_K_HEADS):
            qc = xc[:, kh * dk:(kh + 1) * dk]
            kc = xc[:, GDN_QK_DIM + kh * dk:GDN_QK_DIM + (kh + 1) * dk]
            qn = qc * lax.rsqrt(jnp.sum(qc * qc, axis=-1, keepdims=True) + EPS) * (dk ** -0.5)
            kn = kc * lax.rsqrt(jnp.sum(kc * kc, axis=-1, keepdims=True) + EPS)
            kk = _dot(kn, kn, _NT)
            qk = _dot(qn, kn, _NT)
            for j in range(kh * rep, (kh + 1) * rep):
                col = g_cum[:, j:j + 1]
                seg = col - g_cum_t[j:j + 1, :]
                decay = jnp.exp(jnp.where(tril, seg, NEG_BIG))
                beta = beta_all[:, j:j + 1]
                e_col = jnp.exp(col)
                g_last = g_cum[q - 1:q, j:j + 1]
                vj = xc[:, 2 * GDN_QK_DIM + j * dv:2 * GDN_QK_DIM + (j + 1) * dv]
                heads.append(dict(
                    s=s, j=j, rows=rows,
                    m=jnp.where(strict, beta * kk * decay, 0.0),
                    rhs=jnp.concatenate([vj * beta, kn * (beta * e_col)], axis=1),
                    attn=qk * decay, q_dec=qn * e_col, k_dec=kn * jnp.exp(g_last - col),
                    s_dec=jnp.exp(g_last)))

    t_invs = _unit_lower_inverse([hd["m"] for hd in heads], q)
    sols = [_dot(t_inv, hd["rhs"]) for t_inv, hd in zip(t_invs, heads)]
    states = [s_scr[hd["s"], dk * hd["j"]:dk * (hd["j"] + 1), :] for hd in heads]
    boths = [_dot(jnp.concatenate([sol[:, dv:], hd["q_dec"]], axis=0), st)
             for sol, hd, st in zip(sols, heads, states)]
    v_news = [sol[:, :dv] - both[:q] for sol, both in zip(sols, boths)]
    outs = [both[q:] + _dot(hd["attn"], v_new) for both, hd, v_new in zip(boths, heads, v_news)]
    for hd, st, v_new in zip(heads, states, v_news):
        s_scr[hd["s"], dk * hd["j"]:dk * (hd["j"] + 1), :] = st * hd["s_dec"] + _dot(hd["k_dec"], v_new, _TN)
    for s in range(nseq):
        rows = slice(s * q, (s + 1) * q)
        ys = []
        for hd, o in zip(heads, outs):
            if hd["s"] == s:
                o = o * lax.rsqrt(jnp.mean(o * o, axis=-1, keepdims=True) + EPS) * nw_ref[...]
                ys.append(o * _silu(z_ref[rows, dv * hd["j"]:dv * (hd["j"] + 1)]))
        y_ref[rows, :] = jnp.concatenate(ys, axis=1).astype(y_ref.dtype)

        @pl.when(c == nc - 1)
        def _():
            sto_ref[s] = s_scr[s]


def _gdn_call(proj, lw, conv_state, gdn_state, layer, *, row0, bsz, seq, q, nseq):
    nc = seq // q
    r = nseq * q
    assert nseq == 1 or nc == 1
    rb0 = row0 // r
    hd = GDN_HEAD
    sw = GDN_V_HEADS * hd
    grid = (bsz // nseq, nc)
    rowi = lambda b, c: rb0 + b * nc + c
    full = lambda shape: pl.BlockSpec(shape, lambda b, c: (0,) * len(shape))

    in_specs = [
        pl.BlockSpec((r, GDN_CONV_DIM), lambda b, c: (rowi(b, c), COL_QKV // GDN_CONV_DIM)),
        pl.BlockSpec((r, GDN_WIDTH), lambda b, c: (rowi(b, c), COL_Z_GDN // GDN_WIDTH)),
        pl.BlockSpec((r, 128), lambda b, c: (rowi(b, c), COL_SMALL // 128)),
        full((CONV_WIDTH, GDN_CONV_DIM)), full((1, GDN_V_HEADS)), full((1, GDN_V_HEADS)), full((1, hd)),
        pl.BlockSpec((None, nseq, CONV_WIDTH - 1, GDN_CONV_DIM), lambda b, c: (layer, b, 0, 0)),
        pl.BlockSpec((None, nseq, sw, hd), lambda b, c: (layer, b, 0, 0)),
    ]
    args = [proj, proj, proj, lw["w_conv_gdn"], lw["gdn_dt_bias"].reshape(1, GDN_V_HEADS),
            lw["gdn_a_log"].reshape(1, GDN_V_HEADS), lw["gdn_norm"].reshape(1, hd), conv_state, gdn_state]
    out_shape = [
        jax.ShapeDtypeStruct((bsz * seq, GDN_WIDTH), BF16),
        jax.ShapeDtypeStruct((bsz, CONV_WIDTH - 1, GDN_CONV_DIM), F32),
        jax.ShapeDtypeStruct((bsz, sw, hd), F32),
    ]
    out_specs = [
        pl.BlockSpec((r, GDN_WIDTH), lambda b, c: (b * nc + c, 0)),
        pl.BlockSpec((nseq, CONV_WIDTH - 1, GDN_CONV_DIM), lambda b, c: (b, 0, 0)),
        pl.BlockSpec((nseq, sw, hd), lambda b, c: (b, 0, 0)),
    ]
    scratch = [pltpu.VMEM((nseq, q + 8, GDN_CONV_DIM), F32), pltpu.VMEM((nseq, sw, hd), F32)]
    kern = functools.partial(_gdn_kernel, q=q, nseq=nseq, nc=nc)
    return pl.pallas_call(
        kern, grid=grid, in_specs=in_specs, out_specs=out_specs, out_shape=out_shape,
        scratch_shapes=scratch,
        compiler_params=_cparams(("parallel", "arbitrary")), name="gdn_mixer",
    )(*args)


def _ffn_kernel(te_ref, xi_ref, nv_ref, x_ref, wg_ref, wu_ref, wd_ref, o_ref, *scratch):
    i = pl.program_id(0)
    j = pl.program_id(1)
    valid = i < nv_ref[0]

    @pl.when(j == 0)
    def _():
        o_ref[...] = jnp.zeros(o_ref.shape, o_ref.dtype)
        if scratch:
            scratch[0][...] = x_ref[...].astype(BF16)

    @pl.when(valid)
    def _():
        x = scratch[0][...] if scratch else x_ref[...]
        gate = jnp.dot(x, wg_ref[0].astype(BF16), preferred_element_type=F32)
        up = jnp.dot(x, wu_ref[0].astype(BF16), preferred_element_type=F32)
        act = (_silu(gate) * up).astype(BF16)
        o_ref[...] += jnp.dot(act, wd_ref[0].astype(BF16), preferred_element_type=F32)


def _ffn_call(x, w_gate, w_up, w_down, tile_expert, tile_src, n_valid, tm):
    tp, d = x.shape
    f = w_gate.shape[2]
    tf = FFN_TILE_F
    nf = f // tf
    nt = tp // tm

    def jeff(i, j, nv):
        return jnp.where(i < nv[0], j, nf - 1)

    grid_spec = pltpu.PrefetchScalarGridSpec(
        num_scalar_prefetch=3,
        grid=(nt, nf),
        in_specs=[
            pl.BlockSpec((tm, d), lambda i, j, te, xi, nv: (xi[i], 0)),
            pl.BlockSpec((1, d, tf), lambda i, j, te, xi, nv: (te[i], 0, jeff(i, j, nv))),
            pl.BlockSpec((1, d, tf), lambda i, j, te, xi, nv: (te[i], 0, jeff(i, j, nv))),
            pl.BlockSpec((1, tf, d), lambda i, j, te, xi, nv: (te[i], jeff(i, j, nv), 0)),
        ],
        out_specs=pl.BlockSpec((tm, d), lambda i, j, te, xi, nv: (i, 0)),
        scratch_shapes=[] if x.dtype == BF16 else [pltpu.VMEM((tm, d), BF16)],
    )
    return pl.pallas_call(
        _ffn_kernel, grid_spec=grid_spec, out_shape=jax.ShapeDtypeStruct((tp, d), F32),
        compiler_params=_cparams(("arbitrary", "arbitrary")), name="grouped_swiglu",
    )(tile_expert, tile_src, n_valid, x, w_gate, w_up, w_down)


def _router_kernel(h_ref, wr_ref, sel_ref, cw_ref):
    logits = _dot6(wr_ref[...], h_ref[...], _NT)
    ne = logits.shape[0]
    ei = _iota(logits.shape, 0)
    m1 = jnp.max(logits, axis=0, keepdims=True)
    i1 = jnp.min(jnp.where(logits == m1, ei, ne), axis=0, keepdims=True)
    rest = jnp.where(ei == i1, -jnp.inf, logits)
    m2 = jnp.max(rest, axis=0, keepdims=True)
    i2 = jnp.min(jnp.where(rest == m2, ei, ne), axis=0, keepdims=True)
    e = jnp.exp(m2 - m1)
    w1 = 1.0 / (1.0 + e)
    w2 = e / (1.0 + e)
    sel_ref[...] = jnp.where(ei == i1, 1, jnp.where(ei == i2, 2, 0)).astype(jnp.int32)
    cw_ref[...] = jnp.where(ei == i1, w1, jnp.where(ei == i2, w2, 0.0))


def _router_call(h, w_router):
    t, d = h.shape
    ne = w_router.shape[1]
    tm = TOKEN_TILE
    return pl.pallas_call(
        _router_kernel, grid=(t // tm,),
        in_specs=[pl.BlockSpec((tm, d), lambda i: (i, 0)), pl.BlockSpec((ne, d), lambda i: (0, 0))],
        out_specs=[pl.BlockSpec((ne, tm), lambda i: (0, i)), pl.BlockSpec((ne, tm), lambda i: (0, i))],
        out_shape=[jax.ShapeDtypeStruct((ne, t), jnp.int32), jax.ShapeDtypeStruct((ne, t), F32)],
        compiler_params=_cparams(("parallel",)), name="router_top2",
    )(h, w_router.T)


def _gather_kernel(*refs, n_src, tm):
    idx_now = refs[:n_src]
    idx_next = refs[n_src:2 * n_src]
    table, w_ref, o_ref, buf, sem = refs[2 * n_src:]
    i = pl.program_id(0)

    def issue(idx_refs, slot):
        for k in range(n_src):
            def body(r8, carry, k=k):
                for u in range(GATHER_UNROLL):
                    r = r8 * GATHER_UNROLL + u
                    pltpu.make_async_copy(table.at[pl.ds(idx_refs[k][0, 0, r], 1), :],
                                          buf.at[slot, k, pl.ds(r, 1), :], sem.at[slot, k]).start()
                return carry
            lax.fori_loop(0, tm // GATHER_UNROLL, body, 0)

    slot = lax.rem(i, 2)

    @pl.when(i == 0)
    def _():
        issue(idx_now, 0)

    @pl.when(i + 1 < pl.num_programs(0))
    def _():
        issue(idx_next, 1 - slot)

    acc = None
    for k in range(n_src):
        pltpu.make_async_copy(table.at[pl.ds(0, tm), :], buf.at[slot, k], sem.at[slot, k]).wait()
        term = buf[slot, k] * w_ref[:, k:k + 1]
        acc = term if acc is None else acc + term
    o_ref[...] = acc


def _gather_call(table, idx, weights):
    n_src, t_out = idx.shape
    d = table.shape[1]
    tm = GATHER_TILE
    nt = t_out // tm
    idx3 = idx.reshape(n_src, nt, 1, tm)
    smem = functools.partial(pl.BlockSpec, memory_space=pltpu.SMEM)
    in_specs = ([smem((1, 1, tm), lambda i: (i, 0, 0)) for _ in range(n_src)]
                + [smem((1, 1, tm), lambda i: (jnp.minimum(i + 1, nt - 1), 0, 0)) for _ in range(n_src)]
                + [pl.BlockSpec(memory_space=pl.ANY), pl.BlockSpec((tm, n_src), lambda i: (i, 0))])
    return pl.pallas_call(
        functools.partial(_gather_kernel, n_src=n_src, tm=tm), grid=(nt,), in_specs=in_specs,
        out_specs=pl.BlockSpec((tm, d), lambda i: (i, 0)),
        out_shape=jax.ShapeDtypeStruct((t_out, d), table.dtype),
        scratch_shapes=[pltpu.VMEM((2, n_src, tm, d), table.dtype), pltpu.SemaphoreType.DMA((2, n_src))],
        compiler_params=_cparams(("arbitrary",)), name="row_gather",
    )(*([idx3[k] for k in range(n_src)] * 2), table, weights)


def _scatter_kernel(*refs, n_dst, tm):
    idx_refs = refs[:n_dst]
    table, _, out, sem = refs[n_dst:]
    i = pl.program_id(0)
    slot = lax.rem(i, 2)

    def wait_tile(sl):
        for _ in range(n_dst):
            pltpu.make_async_copy(table.at[pl.ds(0, tm), :], out.at[pl.ds(0, tm), :], sem.at[sl]).wait()

    for k in range(n_dst):
        def body(r8, carry, k=k):
            for u in range(GATHER_UNROLL):
                r = r8 * GATHER_UNROLL + u
                pltpu.make_async_copy(table.at[pl.ds(i * tm + r, 1), :],
                                      out.at[pl.ds(idx_refs[k][0, 0, r], 1), :], sem.at[slot]).start()
            return carry
        lax.fori_loop(0, tm // GATHER_UNROLL, body, 0)

    @pl.when(i > 0)
    def _():
        wait_tile(1 - slot)

    @pl.when(i == pl.num_programs(0) - 1)
    def _():
        wait_tile(slot)


def _scatter_rows_call(table, idx, n_out):
    n_dst, t = idx.shape
    d = table.shape[1]
    tm = GATHER_TILE
    nt = t // tm
    idx3 = idx.reshape(n_dst, nt, 1, tm)
    smem = functools.partial(pl.BlockSpec, memory_space=pltpu.SMEM)
    hbm = pl.BlockSpec(memory_space=pl.ANY)
    return pl.pallas_call(
        functools.partial(_scatter_kernel, n_dst=n_dst, tm=tm), grid=(nt,),
        in_specs=[smem((1, 1, tm), lambda i: (i, 0, 0)) for _ in range(n_dst)] + [hbm, hbm],
        out_specs=hbm,
        out_shape=jax.ShapeDtypeStruct((n_out, d), table.dtype),
        scratch_shapes=[pltpu.SemaphoreType.DMA((2,))],
        input_output_aliases={n_dst + 1: 0},
        compiler_params=_cparams(("arbitrary",)), name="row_scatter",
    )(*[idx3[k] for k in range(n_dst)], table, jnp.zeros((n_out, d), table.dtype))


def _moe(h, w_router, w_gate, w_up, w_down):
    t, d = h.shape
    ne = w_router.shape[1]
    tm = FFN_TILE_M
    sel, cw = _router_call(h, w_router)
    hot = (sel > 0).astype(jnp.int32)
    rank = jnp.cumsum(hot, axis=1) - hot
    counts = jnp.sum(hot, axis=1)
    tiles_e = (counts + tm - 1) // tm
    tile_end = jnp.cumsum(tiles_e)
    tile_start = tile_end - tiles_e
    n_valid = tile_end[-1]
    n_tiles = (2 * t + tm - 1) // tm + ne
    tp = n_tiles * tm
    dest = tile_start[:, None] * tm + rank
    tid = jnp.arange(n_tiles, dtype=jnp.int32)
    te = jnp.sum((tid[:, None] >= tile_end[None, :]).astype(jnp.int32), axis=1)
    last = jnp.maximum(n_valid - 1, 0)
    te = jnp.where(tid < n_valid, te, te[last]).astype(jnp.int32)
    xi = jnp.minimum(tid, last).astype(jnp.int32)
    p1 = jnp.sum(jnp.where(sel == 1, dest, 0), axis=0)
    p2 = jnp.sum(jnp.where(sel == 2, dest, 0), axis=0)
    wts = jnp.stack([jnp.sum(jnp.where(sel == 1, cw, 0.0), axis=0),
                     jnp.sum(jnp.where(sel == 2, cw, 0.0), axis=0)], axis=1)

    pidx = jnp.stack([p1, p2]).astype(jnp.int32)
    xs = _scatter_rows_call(h, pidx, tp)
    ys = _ffn_call(xs, w_gate, w_up, w_down, te, xi, n_valid.reshape(1).astype(jnp.int32), tm)
    return _gather_call(ys, pidx, wts)


_SRC_Z_SSD = 0
_SRC_X_SSD = SSD_WIDTH
_SRC_BC = 2 * SSD_WIDTH
_SRC_DT = SSD_WIDTH + SSD_CONV_DIM
_SRC_QKV = _SRC_DT + SSD_HEADS
_SRC_Z_GDN = _SRC_QKV + GDN_CONV_DIM
_SRC_A = _SRC_Z_GDN + GDN_WIDTH
_SRC_B = _SRC_A + GDN_V_HEADS
_SEGMENTS = ((COL_QKV, _SRC_QKV, GDN_CONV_DIM), (COL_Z_SSD, _SRC_Z_SSD, SSD_WIDTH),
             (COL_Z_GDN, _SRC_Z_GDN, GDN_WIDTH), (COL_X_SSD, _SRC_X_SSD, SSD_WIDTH),
             (COL_BC, _SRC_BC, SSD_CONV_DIM - SSD_WIDTH))


def _reorder_kernel(w_ref, small_ref, o_ref):
    w = w_ref[0]
    for dst, src, width in _SEGMENTS:
        o_ref[0, :, dst:dst + width] = w[:, src:src + width].astype(BF16)
    o_ref[0, :, COL_SMALL:PROJ_COLS] = small_ref[0]


def _reorder_in_proj(w_in):
    depth, d, n_in = w_in.shape
    tk = 256
    pad = jnp.zeros((depth, d, PROJ_COLS - COL_SMALL - SSD_HEADS - 2 * GDN_V_HEADS), w_in.dtype)
    small = jnp.concatenate([w_in[:, :, _SRC_DT:_SRC_DT + SSD_HEADS], w_in[:, :, _SRC_A:_SRC_A + GDN_V_HEADS],
                             w_in[:, :, _SRC_B:_SRC_B + GDN_V_HEADS], pad], axis=-1).astype(BF16)
    return pl.pallas_call(
        _reorder_kernel, grid=(depth, d // tk),
        in_specs=[pl.BlockSpec((1, tk, n_in), lambda l, k: (l, k, 0)),
                  pl.BlockSpec((1, tk, PROJ_COLS - COL_SMALL), lambda l, k: (l, k, 0))],
        out_specs=pl.BlockSpec((1, tk, PROJ_COLS), lambda l, k: (l, k, 0)),
        out_shape=jax.ShapeDtypeStruct((depth, d, PROJ_COLS), BF16),
        compiler_params=_cparams(("parallel", "parallel")), name="reorder_w_in",
    )(w_in, small)


def kernel(x_prompt, x_sample, c_prompt, c_sample, state_ssd_conv, state_ssm, state_gdn_conv, state_gdn, w_ada, b_ada, w_norm_mix, w_norm_ffn, w_in, w_conv_ssd, b_conv_ssd, ssd_dt_bias, ssd_a_log, ssd_d, ssd_norm, w_conv_gdn, gdn_dt_bias, gdn_a_log, gdn_norm, w_out, w_ffn_gate, w_ffn_up, w_ffn_down, w_router, w_exp_gate, w_exp_up, w_exp_down, w_norm_final):
    bp, lp, d = x_prompt.shape
    bs, ls, _ = x_sample.shape
    depth = w_in.shape[0]
    tpr = bp * lp
    tsa = bs * ls
    t = tpr + tsa

    x = (x_prompt.reshape(tpr, d), x_sample.reshape(tsa, d))
    mod = _ada_call(jnp.concatenate([c_sample, c_prompt], axis=0), w_ada, b_ada)
    w_in_r = _reorder_in_proj(w_in)
    w_out_b = w_out.astype(BF16)

    qp = PROMPT_CHUNK if lp % PROMPT_CHUNK == 0 else lp
    nseq_s = 16 // ls if ls < 16 else 1
    sw = GDN_V_HEADS * GDN_HEAD
    zeros_p = (jnp.zeros((1, bp, CONV_WIDTH - 1, SSD_CONV_DIM), F32), jnp.zeros((1, bp, SSD_WIDTH, SSD_STATE), F32),
               jnp.zeros((1, bp, CONV_WIDTH - 1, GDN_CONV_DIM), F32), jnp.zeros((1, bp, sw, GDN_HEAD), F32))
    ssm_in = state_ssm.reshape(depth, bs, SSD_WIDTH, SSD_STATE)
    gdn_in = state_gdn.reshape(depth, bs, sw, GDN_HEAD)

    new_p = [[], [], [], []]
    new_s = [[], [], [], []]
    common = dict(t_prompt=tpr, l_prompt=lp)
    mods = [(mod[l, bs:].reshape(bp, 1, N_MOD * d), jnp.repeat(mod[l, :bs], ls, axis=0)) for l in range(depth)]
    h = _ew_call(x, w_norm_mix[0], mod=(1, 0) + mods[0], **common)
    for l in range(depth):
        mod_p, mod_s = mods[l]
        lw = {"w_conv_ssd": w_conv_ssd[l], "b_conv_ssd": b_conv_ssd[l], "ssd_dt_bias": ssd_dt_bias[l],
              "ssd_a_log": ssd_a_log[l], "ssd_d": ssd_d[l], "ssd_norm": ssd_norm[l],
              "w_conv_gdn": w_conv_gdn[l], "gdn_dt_bias": gdn_dt_bias[l], "gdn_a_log": gdn_a_log[l],
              "gdn_norm": gdn_norm[l]}
        proj = _mm_call(h, w_in_r[l], IN_PROJ_TILE_N)

        prompt = dict(row0=0, bsz=bp, seq=lp, q=qp, nseq=1)
        sample = dict(row0=tpr, bsz=bs, seq=ls, q=ls, nseq=nseq_s)
        yp_ssd, pc, pst = _ssd_call(proj, lw, zeros_p[0], zeros_p[1], 0, **prompt)
        ys_ssd, sc_, sst = _ssd_call(proj, lw, state_ssd_conv, ssm_in, l, **sample)
        yp_gdn, pgc, pgst = _gdn_call(proj, lw, zeros_p[2], zeros_p[3], 0, **prompt)
        ys_gdn, sgc, sgst = _gdn_call(proj, lw, state_gdn_conv, gdn_in, l, **sample)
        for lst, v in zip(new_p, (pc, pst, pgc, pgst)):
            lst.append(v)
        for lst, v in zip(new_s, (sc_, sst, sgc, sgst)):
            lst.append(v)

        mix = _out_proj_call(yp_ssd, yp_gdn, ys_ssd, ys_gdn, w_out_b[l])
        moe_layer = l % 2 == 1
        x, h2 = _ew_call(x, w_norm_ffn[l], res=(mix, 2, mod_p, mod_s), mod=(4, 3, mod_p, mod_s), out_x=True,
                         h_dtype=F32 if moe_layer else BF16, **common)
        i = l // 2
        if moe_layer:
            f = _moe(h2, w_router[i], w_exp_gate[i], w_exp_up[i], w_exp_down[i])
        else:
            nt = t // FFN_TILE_M
            tid = jnp.arange(nt, dtype=jnp.int32)
            f = _ffn_call(h2, w_ffn_gate[i][None], w_ffn_up[i][None], w_ffn_down[i][None],
                          jnp.zeros((nt,), jnp.int32), tid, jnp.full((1,), nt, jnp.int32), FFN_TILE_M)
        if l + 1 < depth:
            x, h = _ew_call(x, w_norm_mix[l + 1], res=(f, 5, mod_p, mod_s), mod=(1, 0) + mods[l + 1],
                            out_x=True, **common)
        else:
            y_p, y_s = _ew_call(x, w_norm_final, res=(f, 5, mod_p, mod_s), h_dtype=F32, split_out=True, **common)

    def stack_states(lst, bsz):
        conv_s = jnp.stack(lst[0])
        ssm = jnp.stack(lst[1]).reshape(depth, bsz, SSD_HEADS, SSD_HEAD_DIM, SSD_STATE)
        conv_g = jnp.stack(lst[2])
        gdn = jnp.stack(lst[3]).reshape(depth, bsz, GDN_V_HEADS, GDN_HEAD, GDN_HEAD)
        return conv_s, ssm, conv_g, gdn

    return ((y_p.reshape(bp, lp, d), y_s.reshape(bs, ls, d))
            + stack_states(new_p, bp) + stack_states(new_s, bs))
```

```python
import functools

import jax
import jax.numpy as jnp
from jax import lax
from jax.experimental import pallas as pl
from jax.experimental.pallas import tpu as pltpu

F32 = jnp.float32
BF16 = jnp.bfloat16

D_MODEL = 2048
CONV_WIDTH = 4
SSD_WIDTH = 1024
SSD_HEAD_DIM = 64
SSD_HEADS = 16
SSD_GROUPS = 2
SSD_STATE = 128
SSD_CONV_DIM = SSD_WIDTH + 2 * SSD_GROUPS * SSD_STATE
GDN_WIDTH = 1024
GDN_HEAD = 128
GDN_V_HEADS = 8
GDN_K_HEADS = 4
GDN_QK_DIM = GDN_K_HEADS * GDN_HEAD
GDN_CONV_DIM = 2 * GDN_QK_DIM + GDN_WIDTH
N_EXPERTS = 8
N_MOD = 6
EPS = 1e-6

PROJ_COLS = 5760
COL_QKV, COL_Z_SSD, COL_Z_GDN, COL_X_SSD, COL_BC, COL_SMALL = 0, 2048, 3072, 4096, 5120, 5632
SMALL_DT, SMALL_A, SMALL_B = 0, SSD_HEADS, SSD_HEADS + GDN_V_HEADS

V7X_VMEM_LIMIT = 56 * 1024 * 1024
TOKEN_TILE = 256
MM_TILE_M = 1024
IN_PROJ_TILE_N = 1920
FFN_TILE_M = 768
FFN_TILE_F = 512
GATHER_TILE = 256
GATHER_UNROLL = 8
PROMPT_CHUNK = 128
NEG_BIG = -1e30


def _cparams(sem):
    return pltpu.CompilerParams(dimension_semantics=sem, vmem_limit_bytes=V7X_VMEM_LIMIT)


_NN = (((1,), (0,)), ((), ()))
_NT = (((1,), (1,)), ((), ()))
_TN = (((0,), (0,)), ((), ()))


def _dg(a, b, dims):
    return lax.dot_general(a, b, dims, preferred_element_type=F32)


def _dot(a, b, dims=_NN):
    return _dg(a.astype(BF16), b.astype(BF16), dims)


def _split(x, n):
    parts = []
    r = x
    for _ in range(n - 1):
        p = r.astype(BF16)
        parts.append(p)
        r = r - p.astype(F32)
    parts.append(r.astype(BF16))
    return parts


def _dot_exact_rhs(a, b, dims=_NN):
    bb = b.astype(BF16)
    a1, a2, a3 = _split(a, 3)
    return _dg(a3, bb, dims) + _dg(a2, bb, dims) + _dg(a1, bb, dims)


def _dot_exact_lhs(a, b, dims=_NN):
    ab = a.astype(BF16)
    b1, b2, b3 = _split(b, 3)
    return _dg(ab, b3, dims) + _dg(ab, b2, dims) + _dg(ab, b1, dims)


def _dot6(a, b, dims=_NN):
    a1, a2, a3 = _split(a, 3)
    b1, b2, b3 = _split(b, 3)
    small = _dg(a1, b3, dims) + _dg(a2, b2, dims) + _dg(a3, b1, dims)
    mid = _dg(a1, b2, dims) + _dg(a2, b1, dims)
    return small + mid + _dg(a1, b1, dims)


def _sigmoid(x):
    return 1.0 / (1.0 + jnp.exp(-x))


def _silu(x):
    return x * _sigmoid(x)


def _softplus(x):
    return jnp.maximum(x, 0.0) + jnp.log1p(jnp.exp(-jnp.abs(x)))


def _iota(shape, dim):
    return lax.broadcasted_iota(jnp.int32, shape, dim)


def _transpose_cols(x, n):
    eye = (_iota((n, n), 0) == _iota((n, n), 1)).astype(F32)
    return _dot_exact_lhs(eye, x, _NT)


def _conv_silu(u, ext_ref, s, w_ref, bias, q):
    ext_ref[s, 8:8 + q, :] = u
    acc = u * w_ref[CONV_WIDTH - 1:CONV_WIDTH, :]
    if bias is not None:
        acc = acc + bias
    for j in range(1, CONV_WIDTH):
        acc = acc + ext_ref[s, 8 - j:8 - j + q, :] * w_ref[CONV_WIDTH - 1 - j:CONV_WIDTH - j, :]
    return _silu(acc)


def _conv_advance(ext_ref, s, q):
    ext_ref[s, 0:8, :] = ext_ref[s, q:q + 8, :]


def _conv_init(ext_ref, s, state_rows):
    ext_ref[s, 0:8, :] = jnp.zeros((8, ext_ref.shape[-1]), F32)
    ext_ref[s, 8 - (CONV_WIDTH - 1):8, :] = state_rows


def _conv_tail(ext_ref, s, q):
    return ext_ref[s, q + 8 - (CONV_WIDTH - 1):q + 8, :]


def _ada_kernel(c_ref, w_ref, b_ref, o_ref):
    a = _silu(c_ref[...]).astype(BF16)
    o_ref[0] = jnp.dot(a, w_ref[0].astype(BF16), preferred_element_type=F32) + b_ref[0]


def _ada_call(c_all, w_ada, b_ada):
    depth, d, n = w_ada.shape
    m = c_all.shape[0]
    tn = 1024
    return pl.pallas_call(
        _ada_kernel,
        grid=(depth, n // tn),
        in_specs=[
            pl.BlockSpec((m, d), lambda l, j: (0, 0)),
            pl.BlockSpec((1, d, tn), lambda l, j: (l, 0, j)),
            pl.BlockSpec((1, 1, tn), lambda l, j: (l, 0, j)),
        ],
        out_specs=pl.BlockSpec((1, m, tn), lambda l, j: (l, 0, j)),
        out_shape=jax.ShapeDtypeStruct((depth, m, n), F32),
        compiler_params=_cparams(("parallel", "parallel")),
        name="ada_mod",
    )(c_all, w_ada, b_ada.reshape(depth, 1, n))


def _ew_kernel(*refs, split_in, has_res, modded, out_x, split_out, n_prompt_tiles):
    refs = list(refs)
    xp_ref = refs.pop(0)
    xs_ref = refs.pop(0) if split_in else xp_ref
    if has_res:
        y_ref, gp_ref, gs_ref = refs.pop(0), refs.pop(0), refs.pop(0)
    w_ref = refs.pop(0)
    if modded:
        scp_ref, shp_ref, scs_ref, shs_ref = refs.pop(0), refs.pop(0), refs.pop(0), refs.pop(0)
    if out_x:
        xo_ref = refs.pop(0)
    hp_ref = refs.pop(0)
    hs_ref = refs.pop(0) if split_out else hp_ref
    i = pl.program_id(0)

    def run(prompt):
        x = (xp_ref if prompt else xs_ref)[...]
        if has_res:
            g = gp_ref[0] if prompt else gs_ref[...]
            x = x + g * y_ref[...]
            if out_x:
                xo_ref[...] = x
        h = x * lax.rsqrt(jnp.mean(x * x, axis=-1, keepdims=True) + EPS) * w_ref[...]
        if modded:
            sc = scp_ref[0] if prompt else scs_ref[...]
            sh = shp_ref[0] if prompt else shs_ref[...]
            h = h * (1.0 + sc) + sh
        h_ref = hp_ref if prompt else hs_ref
        h_ref[...] = h.astype(h_ref.dtype)

    @pl.when(i < n_prompt_tiles)
    def _():
        run(True)

    @pl.when(i >= n_prompt_tiles)
    def _():
        run(False)


def _ew_call(x, w, *, t_prompt, l_prompt, res=None, mod=None, out_x=False, h_dtype=BF16, split_out=False):
    split_in = isinstance(x, tuple)
    d = w.shape[0]
    t = sum(a.shape[0] for a in x) if split_in else x.shape[0]
    tm = TOKEN_TILE
    npt = t_prompt // tm
    per_seq = l_prompt // tm

    def pspec(chunk, mp):
        bp = mp.shape[0]
        return pl.BlockSpec((1, 1, d), lambda i: (jnp.minimum(i // per_seq, bp - 1), 0, chunk))

    def sspec(chunk):
        return pl.BlockSpec((tm, d), lambda i: (jnp.maximum(i - npt, 0), chunk))

    row = pl.BlockSpec((tm, d), lambda i: (i, 0))
    prow = pl.BlockSpec((tm, d), lambda i: (jnp.minimum(i, npt - 1), 0))
    srow = sspec(0)
    args, specs = (list(x), [prow, srow]) if split_in else ([x], [row])
    if res is not None:
        y, gch, gmp, gms = res
        args += [y, gmp, gms]
        specs += [row, pspec(gch, gmp), sspec(gch)]
    args.append(w.reshape(1, d))
    specs.append(pl.BlockSpec((1, d), lambda i: (0, 0)))
    if mod is not None:
        scc, shc, mmp, mms = mod
        args += [mmp, mmp, mms, mms]
        specs += [pspec(scc, mmp), pspec(shc, mmp), sspec(scc), sspec(shc)]
    out_shape, out_specs = [], []
    if out_x:
        out_shape.append(jax.ShapeDtypeStruct((t, d), F32))
        out_specs.append(row)
    if split_out:
        out_shape += [jax.ShapeDtypeStruct((t_prompt, d), h_dtype), jax.ShapeDtypeStruct((t - t_prompt, d), h_dtype)]
        out_specs += [prow, srow]
    else:
        out_shape.append(jax.ShapeDtypeStruct((t, d), h_dtype))
        out_specs.append(row)
    kern = functools.partial(_ew_kernel, split_in=split_in, has_res=res is not None, modded=mod is not None,
                             out_x=out_x, split_out=split_out, n_prompt_tiles=npt)
    outs = pl.pallas_call(
        kern, grid=(t // tm,), in_specs=specs, out_specs=out_specs, out_shape=out_shape,
        compiler_params=_cparams(("arbitrary",)), name="token_norm",
    )(*args)
    return outs if len(outs) > 1 else outs[0]


def _in_proj_kernel(a_ref, wt_ref, o_ref):
    o_ref[...] = _dg(a_ref[...], wt_ref[...], _NT)


def _in_proj_call(a, w_t, layer):
    m, k = a.shape
    n = w_t.shape[1]
    tn = IN_PROJ_TILE_N
    tm = MM_TILE_M
    while m % tm:
        tm //= 2
    return pl.pallas_call(
        _in_proj_kernel,
        grid=(m // tm, n // tn),
        in_specs=[pl.BlockSpec((tm, k), lambda i, j: (i, 0)),
                  pl.BlockSpec((None, tn, k), lambda i, j: (layer, j, 0))],
        out_specs=pl.BlockSpec((tm, tn), lambda i, j: (i, j)),
        out_shape=jax.ShapeDtypeStruct((m, n), F32),
        compiler_params=_cparams(("parallel", "parallel")),
        name="in_proj",
    )(a, w_t)


def _out_proj_kernel(ap1_ref, ap2_ref, as1_ref, as2_ref, w1_ref, w2_ref, o_ref, *, n_prompt_tiles):
    i = pl.program_id(0)

    def run(a1_ref, a2_ref):
        o_ref[...] = (jnp.dot(a1_ref[...], w1_ref[...], preferred_element_type=F32)
                      + jnp.dot(a2_ref[...], w2_ref[...], preferred_element_type=F32))

    @pl.when(i < n_prompt_tiles)
    def _():
        run(ap1_ref, ap2_ref)

    @pl.when(i >= n_prompt_tiles)
    def _():
        run(as1_ref, as2_ref)


def _out_proj_call(yp_ssd, yp_gdn, ys_ssd, ys_gdn, w_out):
    tp, k1 = yp_ssd.shape
    ts = ys_ssd.shape[0]
    k2 = yp_gdn.shape[1]
    n = w_out.shape[1]
    tm = MM_TILE_M
    while tp % tm or ts % tm:
        tm //= 2
    tn = 1024
    npt = tp // tm
    pmap = lambda i, j: (jnp.minimum(i, npt - 1), 0)
    smap = lambda i, j: (jnp.maximum(i - npt, 0), 0)
    return pl.pallas_call(
        functools.partial(_out_proj_kernel, n_prompt_tiles=npt),
        grid=((tp + ts) // tm, n // tn),
        in_specs=[pl.BlockSpec((tm, k1), pmap), pl.BlockSpec((tm, k2), pmap),
                  pl.BlockSpec((tm, k1), smap), pl.BlockSpec((tm, k2), smap),
                  pl.BlockSpec((k1, tn), lambda i, j: (0, j)),
                  pl.BlockSpec((k2, tn), lambda i, j: (k1 // k2, j))],
        out_specs=pl.BlockSpec((tm, tn), lambda i, j: (i, j)),
        out_shape=jax.ShapeDtypeStruct((tp + ts, n), F32),
        compiler_params=_cparams(("parallel", "parallel")),
        name="out_proj",
    )(yp_ssd, yp_gdn, ys_ssd, ys_gdn, w_out, w_out)


def _ssd_kernel(z_ref, xs_ref, bc_ref, sm_ref, wx_ref, wbc_ref, bx_ref, bbc_ref, dtb_ref, alog_ref,
                dexp_ref, nw_ref, cst_ref, st_ref, y_ref, cso_ref, sto_ref, extx, extbc, s_scr,
                *, q, nseq, nc):
    c = pl.program_id(1)
    nh = SSD_HEADS
    hg = SSD_HEADS // SSD_GROUPS
    wg = SSD_WIDTH // SSD_GROUPS
    p = SSD_HEAD_DIM
    n = SSD_STATE
    gn = SSD_GROUPS * n

    li = _iota((q, q), 0)
    si = _iota((q, q), 1)
    tril = si <= li
    tri = tril.astype(F32)
    expand = (_iota((nh, SSD_WIDTH), 1) // p == _iota((nh, SSD_WIDTH), 0)).astype(F32)
    expand_t = (_iota((SSD_WIDTH, nh), 0) // p == _iota((SSD_WIDTH, nh), 1)).astype(F32)
    lane = _iota((q, 128), 1)

    for s in range(nseq):
        rows = slice(s * q, (s + 1) * q)

        @pl.when(c == 0)
        def _():
            _conv_init(extx, s, cst_ref[s, :, 0:SSD_WIDTH])
            _conv_init(extbc, s, cst_ref[s, :, SSD_WIDTH:SSD_CONV_DIM])
            s_scr[s] = st_ref[s]

        xs = _conv_silu(xs_ref[rows, :], extx, s, wx_ref, bx_ref[...], q)
        bcv = _conv_silu(bc_ref[rows, :], extbc, s, wbc_ref, bbc_ref[...], q)

        @pl.when(c == nc - 1)
        def _():
            cso_ref[s, :, 0:SSD_WIDTH] = _conv_tail(extx, s, q)
            cso_ref[s, :, SSD_WIDTH:SSD_CONV_DIM] = _conv_tail(extbc, s, q)

        _conv_advance(extx, s, q)
        _conv_advance(extbc, s, q)

        dt = _softplus(sm_ref[rows, :][:, SMALL_DT:SMALL_DT + nh] + dtb_ref[...])
        a = dt * (-jnp.exp(alog_ref[...]))
        a_cum = _dot_exact_lhs(tri, a)
        a_cum_t = _transpose_cols(a_cum, nh)
        dt_e = _dot_exact_rhs(dt, expand)
        acum_e = _dot_exact_rhs(a_cum, expand)
        xdt = xs * dt_e
        alast_e = acum_e[q - 1:q, :]
        xd = xdt * jnp.exp(alast_e - acum_e)
        alast_b = jnp.broadcast_to(a_cum_t[:, q - 1:q], (nh, 128))
        dec_rows = jnp.exp(_dot_exact_lhs(expand_t, alast_b))

        pieces = []
        for g in range(SSD_GROUPS):
            bm = bcv[:, g * n:(g + 1) * n]
            cm = bcv[:, gn + g * n:gn + (g + 1) * n]
            cb = _dot(cm, bm, _NT)
            s_old = s_scr[s, g * wg:(g + 1) * wg, :]
            pieces.append(_dot(cm, s_old, _NT))
            s_scr[s, g * wg:(g + 1) * wg, :] = (dec_rows[g * wg:(g + 1) * wg, :] * s_old
                                                + _dot(xd[:, g * wg:(g + 1) * wg], bm, _TN))
            for k in range(hg // 2):
                ms = []
                for hh in (g * hg + 2 * k, g * hg + 2 * k + 1):
                    seg = a_cum[:, hh:hh + 1] - a_cum_t[hh:hh + 1, :]
                    ms.append(cb * jnp.exp(jnp.where(tril, seg, NEG_BIG)))
                c0 = g * wg + 2 * p * k
                xp = xdt[:, c0:c0 + 2 * p]
                top = jnp.where(lane < p, xp, 0.0)
                bot = jnp.where(lane >= p, xp, 0.0)
                if q % 128 == 0:
                    piece = _dot(jnp.concatenate(ms, axis=1), jnp.concatenate([top, bot], axis=0))
                else:
                    piece = _dot(ms[0], top) + _dot(ms[1], bot)
                pieces.append(piece)
        y_off = jnp.concatenate([pieces[0], pieces[1 + hg // 2]], axis=1) * jnp.exp(acum_e)
        y_diag = jnp.concatenate(pieces[1:1 + hg // 2] + pieces[2 + hg // 2:], axis=1)
        y = y_off + y_diag + dexp_ref[...] * xs
        y = y * _silu(z_ref[rows, :])
        outs = []
        for g in range(SSD_GROUPS):
            yg = y[:, g * wg:(g + 1) * wg]
            outs.append(yg * lax.rsqrt(jnp.mean(yg * yg, axis=-1, keepdims=True) + EPS))
        y_ref[rows, :] = (jnp.concatenate(outs, axis=1) * nw_ref[...]).astype(y_ref.dtype)

        @pl.when(c == nc - 1)
        def _():
            sto_ref[s] = s_scr[s]


def _ssd_call(proj, lw, conv_state, ssm_state, layer, *, row0, bsz, seq, q, nseq):
    nc = seq // q
    r = nseq * q
    assert nseq == 1 or nc == 1
    rb0 = row0 // r
    n = SSD_STATE
    bcw = SSD_CONV_DIM - SSD_WIDTH
    grid = (bsz // nseq, nc)
    rowi = lambda b, c: rb0 + b * nc + c
    full = lambda shape: pl.BlockSpec(shape, lambda b, c: (0,) * len(shape))

    in_specs = [
        pl.BlockSpec((r, SSD_WIDTH), lambda b, c: (rowi(b, c), COL_Z_SSD // SSD_WIDTH)),
        pl.BlockSpec((r, SSD_WIDTH), lambda b, c: (rowi(b, c), COL_X_SSD // SSD_WIDTH)),
        pl.BlockSpec((r, bcw), lambda b, c: (rowi(b, c), COL_BC // bcw)),
        pl.BlockSpec((r, 128), lambda b, c: (rowi(b, c), COL_SMALL // 128)),
        pl.BlockSpec((CONV_WIDTH, SSD_WIDTH), lambda b, c: (0, 0)),
        pl.BlockSpec((CONV_WIDTH, bcw), lambda b, c: (0, SSD_WIDTH // bcw)),
        pl.BlockSpec((1, SSD_WIDTH), lambda b, c: (0, 0)),
        pl.BlockSpec((1, bcw), lambda b, c: (0, SSD_WIDTH // bcw)),
        full((1, SSD_HEADS)), full((1, SSD_HEADS)), full((1, SSD_WIDTH)), full((1, SSD_WIDTH)),
        pl.BlockSpec((None, nseq, CONV_WIDTH - 1, SSD_CONV_DIM), lambda b, c: (layer, b, 0, 0)),
        pl.BlockSpec((None, nseq, SSD_WIDTH, n), lambda b, c: (layer, b, 0, 0)),
    ]
    cw = lw["w_conv_ssd"]
    cbias = lw["b_conv_ssd"].reshape(1, SSD_CONV_DIM)
    args = [proj, proj, proj, proj, cw, cw, cbias, cbias,
            lw["ssd_dt_bias"].reshape(1, SSD_HEADS), lw["ssd_a_log"].reshape(1, SSD_HEADS),
            jnp.repeat(lw["ssd_d"], SSD_HEAD_DIM).reshape(1, SSD_WIDTH), lw["ssd_norm"].reshape(1, SSD_WIDTH),
            conv_state, ssm_state]
    out_shape = [
        jax.ShapeDtypeStruct((bsz * seq, SSD_WIDTH), BF16),
        jax.ShapeDtypeStruct((bsz, CONV_WIDTH - 1, SSD_CONV_DIM), F32),
        jax.ShapeDtypeStruct((bsz, SSD_WIDTH, n), F32),
    ]
    out_specs = [
        pl.BlockSpec((r, SSD_WIDTH), lambda b, c: (b * nc + c, 0)),
        pl.BlockSpec((nseq, CONV_WIDTH - 1, SSD_CONV_DIM), lambda b, c: (b, 0, 0)),
        pl.BlockSpec((nseq, SSD_WIDTH, n), lambda b, c: (b, 0, 0)),
    ]
    scratch = [pltpu.VMEM((nseq, q + 8, SSD_WIDTH), F32), pltpu.VMEM((nseq, q + 8, bcw), F32),
               pltpu.VMEM((nseq, SSD_WIDTH, n), F32)]
    kern = functools.partial(_ssd_kernel, q=q, nseq=nseq, nc=nc)
    return pl.pallas_call(
        kern, grid=grid, in_specs=in_specs, out_specs=out_specs, out_shape=out_shape,
        scratch_shapes=scratch,
        compiler_params=_cparams(("parallel", "arbitrary")), name="ssd_mixer",
    )(*args)


def _neumann_inverse(xs, nil):
    n = xs[0].shape[0]
    eye = (_iota((n, n), 0) == _iota((n, n), 1)).astype(F32)
    invs = [eye - x for x in xs]
    pws = xs
    k = 2
    while k < nil:
        pws = [_dot(pw, pw) for pw in pws]
        invs = [inv + _dot(inv, pw) for inv, pw in zip(invs, pws)]
        k *= 2
    return invs


def _unit_lower_inverse(ms, q):
    blk = 16
    if q <= blk:
        return _neumann_inverse(ms, q)
    same = (_iota((q, q), 0) // blk) == (_iota((q, q), 1) // blk)
    dgs = [jnp.where(same, m, 0.0) for m in ms]
    dinvs = _neumann_inverse(dgs, blk)
    nns = [_dot(dinv, m - dg) for dinv, m, dg in zip(dinvs, ms, dgs)]
    ninvs = _neumann_inverse(nns, q // blk)
    return [_dot(ninv, dinv) for ninv, dinv in zip(ninvs, dinvs)]


def _gdn_kernel(qkv_ref, z_ref, sm_ref, w_ref, dtb_ref, alog_ref, nw_ref, cst_ref, st_ref,
                y_ref, cso_ref, sto_ref, ext, s_scr, *, q, nseq, nc):
    c = pl.program_id(1)
    nv = GDN_V_HEADS
    rep = GDN_V_HEADS // GDN_K_HEADS
    dk = GDN_HEAD
    dv = GDN_HEAD

    li = _iota((q, q), 0)
    si = _iota((q, q), 1)
    tril = si <= li
    strict = si < li
    tri = tril.astype(F32)

    heads = []
    for s in range(nseq):
        rows = slice(s * q, (s + 1) * q)

        @pl.when(c == 0)
        def _():
            _conv_init(ext, s, cst_ref[s])
            s_scr[s] = st_ref[s]

        xc = _conv_silu(qkv_ref[rows, :], ext, s, w_ref, None, q)

        @pl.when(c == nc - 1)
        def _():
            cso_ref[s] = _conv_tail(ext, s, q)

        _conv_advance(ext, s, q)

        sm = sm_ref[rows, :]
        a_raw = sm[:, SMALL_A:SMALL_A + nv]
        b_raw = sm[:, SMALL_B:SMALL_B + nv]
        gate = -jnp.exp(alog_ref[...]) * _softplus(a_raw + dtb_ref[...])
        g_cum = _dot_exact_lhs(tri, gate)
        g_cum_t = _transpose_cols(g_cum, nv)
        beta_all = _sigmoid(b_raw)

        for kh in range(GDN_K_HEADS):
            qc = xc[:, kh * dk:(kh + 1) * dk]
            kc = xc[:, GDN_QK_DIM + kh * dk:GDN_QK_DIM + (kh + 1) * dk]
            qn = qc * lax.rsqrt(jnp.sum(qc * qc, axis=-1, keepdims=True) + EPS) * (dk ** -0.5)
            kn = kc * lax.rsqrt(jnp.sum(kc * kc, axis=-1, keepdims=True) + EPS)
            kk = _dot(kn, kn, _NT)
            qk = _dot(qn, kn, _NT)
            for j in range(kh * rep, (kh + 1) * rep):
                col = g_cum[:, j:j + 1]
                seg = col - g_cum_t[j:j + 1, :]
                decay = jnp.exp(jnp.where(tril, seg, NEG_BIG))
                beta = beta_all[:, j:j + 1]
                e_col = jnp.exp(col)
                g_last = g_cum[q - 1:q, j:j + 1]
                vj = xc[:, 2 * GDN_QK_DIM + j * dv:2 * GDN_QK_DIM + (j + 1) * dv]
                heads.append(dict(
                    s=s, j=j, rows=rows,
                    m=jnp.where(strict, beta * kk * decay, 0.0),
                    rhs=jnp.concatenate([vj * beta, kn * (beta * e_col)], axis=1),
                    attn=qk * decay, q_dec=qn * e_col, k_dec=kn * jnp.exp(g_last - col),
                    s_dec=jnp.exp(g_last)))

    t_invs = _unit_lower_inverse([hd["m"] for hd in heads], q)
    sols = [_dot(t_inv, hd["rhs"]) for t_inv, hd in zip(t_invs, heads)]
    states = [s_scr[hd["s"], dk * hd["j"]:dk * (hd["j"] + 1), :] for hd in heads]
    boths = [_dot(jnp.concatenate([sol[:, dv:], hd["q_dec"]], axis=0), st)
             for sol, hd, st in zip(sols, heads, states)]
    v_news = [sol[:, :dv] - both[:q] for sol, both in zip(sols, boths)]
    outs = [both[q:] + _dot(hd["attn"], v_new) for both, hd, v_new in zip(boths, heads, v_news)]
    for hd, st, v_new in zip(heads, states, v_news):
        s_scr[hd["s"], dk * hd["j"]:dk * (hd["j"] + 1), :] = st * hd["s_dec"] + _dot(hd["k_dec"], v_new, _TN)
    for s in range(nseq):
        rows = slice(s * q, (s + 1) * q)
        ys = []
        for hd, o in zip(heads, outs):
            if hd["s"] == s:
                o = o * lax.rsqrt(jnp.mean(o * o, axis=-1, keepdims=True) + EPS) * nw_ref[...]
                ys.append(o * _silu(z_ref[rows, dv * hd["j"]:dv * (hd["j"] + 1)]))
        y_ref[rows, :] = jnp.concatenate(ys, axis=1).astype(y_ref.dtype)

        @pl.when(c == nc - 1)
        def _():
            sto_ref[s] = s_scr[s]


def _gdn_call(proj, lw, conv_state, gdn_state, layer, *, row0, bsz, seq, q, nseq):
    nc = seq // q
    r = nseq * q
    assert nseq == 1 or nc == 1
    rb0 = row0 // r
    hd = GDN_HEAD
    sw = GDN_V_HEADS * hd
    grid = (bsz // nseq, nc)
    rowi = lambda b, c: rb0 + b * nc + c
    full = lambda shape: pl.BlockSpec(shape, lambda b, c: (0,) * len(shape))

    in_specs = [
        pl.BlockSpec((r, GDN_CONV_DIM), lambda b, c: (rowi(b, c), COL_QKV // GDN_CONV_DIM)),
        pl.BlockSpec((r, GDN_WIDTH), lambda b, c: (rowi(b, c), COL_Z_GDN // GDN_WIDTH)),
        pl.BlockSpec((r, 128), lambda b, c: (rowi(b, c), COL_SMALL // 128)),
        full((CONV_WIDTH, GDN_CONV_DIM)), full((1, GDN_V_HEADS)), full((1, GDN_V_HEADS)), full((1, hd)),
        pl.BlockSpec((None, nseq, CONV_WIDTH - 1, GDN_CONV_DIM), lambda b, c: (layer, b, 0, 0)),
        pl.BlockSpec((None, nseq, sw, hd), lambda b, c: (layer, b, 0, 0)),
    ]
    args = [proj, proj, proj, lw["w_conv_gdn"], lw["gdn_dt_bias"].reshape(1, GDN_V_HEADS),
            lw["gdn_a_log"].reshape(1, GDN_V_HEADS), lw["gdn_norm"].reshape(1, hd), conv_state, gdn_state]
    out_shape = [
        jax.ShapeDtypeStruct((bsz * seq, GDN_WIDTH), BF16),
        jax.ShapeDtypeStruct((bsz, CONV_WIDTH - 1, GDN_CONV_DIM), F32),
        jax.ShapeDtypeStruct((bsz, sw, hd), F32),
    ]
    out_specs = [
        pl.BlockSpec((r, GDN_WIDTH), lambda b, c: (b * nc + c, 0)),
        pl.BlockSpec((nseq, CONV_WIDTH - 1, GDN_CONV_DIM), lambda b, c: (b, 0, 0)),
        pl.BlockSpec((nseq, sw, hd), lambda b, c: (b, 0, 0)),
    ]
    scratch = [pltpu.VMEM((nseq, q + 8, GDN_CONV_DIM), F32), pltpu.VMEM((nseq, sw, hd), F32)]
    kern = functools.partial(_gdn_kernel, q=q, nseq=nseq, nc=nc)
    return pl.pallas_call(
        kern, grid=grid, in_specs=in_specs, out_specs=out_specs, out_shape=out_shape,
        scratch_shapes=scratch,
        compiler_params=_cparams(("parallel", "arbitrary")), name="gdn_mixer",
    )(*args)


def _ffn_kernel(te_ref, xi_ref, nv_ref, x_ref, wg_ref, wu_ref, wd_ref, o_ref, *scratch):
    i = pl.program_id(0)
    j = pl.program_id(1)
    valid = i < nv_ref[0]

    @pl.when(j == 0)
    def _():
        o_ref[...] = jnp.zeros(o_ref.shape, o_ref.dtype)
        if scratch:
            scratch[0][...] = x_ref[...].astype(BF16)

    @pl.when(valid)
    def _():
        x = scratch[0][...] if scratch else x_ref[...]
        gate = jnp.dot(x, wg_ref[0].astype(BF16), preferred_element_type=F32)
        up = jnp.dot(x, wu_ref[0].astype(BF16), preferred_element_type=F32)
        act = (_silu(gate) * up).astype(BF16)
        o_ref[...] += jnp.dot(act, wd_ref[0].astype(BF16), preferred_element_type=F32)


def _ffn_call(x, w_gate, w_up, w_down, tile_expert, tile_src, n_valid, tm):
    tp, d = x.shape
    f = w_gate.shape[2]
    tf = FFN_TILE_F
    nf = f // tf
    nt = tp // tm

    def jeff(i, j, nv):
        return jnp.where(i < nv[0], j, nf - 1)

    grid_spec = pltpu.PrefetchScalarGridSpec(
        num_scalar_prefetch=3,
        grid=(nt, nf),
        in_specs=[
            pl.BlockSpec((tm, d), lambda i, j, te, xi, nv: (xi[i], 0)),
            pl.BlockSpec((1, d, tf), lambda i, j, te, xi, nv: (te[i], 0, jeff(i, j, nv))),
            pl.BlockSpec((1, d, tf), lambda i, j, te, xi, nv: (te[i], 0, jeff(i, j, nv))),
            pl.BlockSpec((1, tf, d), lambda i, j, te, xi, nv: (te[i], jeff(i, j, nv), 0)),
        ],
        out_specs=pl.BlockSpec((tm, d), lambda i, j, te, xi, nv: (i, 0)),
        scratch_shapes=[] if x.dtype == BF16 else [pltpu.VMEM((tm, d), BF16)],
    )
    return pl.pallas_call(
        _ffn_kernel, grid_spec=grid_spec, out_shape=jax.ShapeDtypeStruct((tp, d), F32),
        compiler_params=_cparams(("arbitrary", "arbitrary")), name="grouped_swiglu",
    )(tile_expert, tile_src, n_valid, x, w_gate, w_up, w_down)


def _router_kernel(h_ref, wr_ref, sel_ref, cw_ref):
    logits = _dot6(wr_ref[...], h_ref[...], _NT)
    ne = logits.shape[0]
    ei = _iota(logits.shape, 0)
    m1 = jnp.max(logits, axis=0, keepdims=True)
    i1 = jnp.min(jnp.where(logits == m1, ei, ne), axis=0, keepdims=True)
    rest = jnp.where(ei == i1, -jnp.inf, logits)
    m2 = jnp.max(rest, axis=0, keepdims=True)
    i2 = jnp.min(jnp.where(rest == m2, ei, ne), axis=0, keepdims=True)
    e = jnp.exp(m2 - m1)
    w1 = 1.0 / (1.0 + e)
    w2 = e / (1.0 + e)
    sel_ref[...] = jnp.where(ei == i1, 1, jnp.where(ei == i2, 2, 0)).astype(jnp.int32)
    cw_ref[...] = jnp.where(ei == i1, w1, jnp.where(ei == i2, w2, 0.0))


def _router_call(h, w_router):
    t, d = h.shape
    ne = w_router.shape[1]
    tm = TOKEN_TILE
    return pl.pallas_call(
        _router_kernel, grid=(t // tm,),
        in_specs=[pl.BlockSpec((tm, d), lambda i: (i, 0)), pl.BlockSpec((ne, d), lambda i: (0, 0))],
        out_specs=[pl.BlockSpec((ne, tm), lambda i: (0, i)), pl.BlockSpec((ne, tm), lambda i: (0, i))],
        out_shape=[jax.ShapeDtypeStruct((ne, t), jnp.int32), jax.ShapeDtypeStruct((ne, t), F32)],
        compiler_params=_cparams(("parallel",)), name="router_top2",
    )(h, w_router.T)


def _gather_kernel(*refs, n_src, weighted, tm):
    nrows_ref = refs[0]
    idx_now = refs[1:1 + n_src]
    idx_next = refs[1 + n_src:1 + 2 * n_src]
    refs = refs[1 + 2 * n_src:]
    table = refs[0]
    w_ref = refs[1] if weighted else None
    o_ref, buf, sem = refs[-3:]
    i = pl.program_id(0)
    nrows = nrows_ref[0]

    def issue(idx_refs, slot):
        for k in range(n_src):
            def body(r8, carry, k=k):
                for u in range(GATHER_UNROLL):
                    r = r8 * GATHER_UNROLL + u
                    pltpu.make_async_copy(table.at[pl.ds(idx_refs[k][0, 0, r], 1), :],
                                          buf.at[slot, k, pl.ds(r, 1), :], sem.at[slot, k]).start()
                return carry
            lax.fori_loop(0, tm // GATHER_UNROLL, body, 0)

    slot = lax.rem(i, 2)
    valid = i * tm < nrows

    @pl.when((i == 0) & valid)
    def _():
        issue(idx_now, 0)

    @pl.when((i + 1) * tm < nrows)
    def _():
        issue(idx_next, 1 - slot)

    @pl.when(valid)
    def _():
        acc = None
        for k in range(n_src):
            pltpu.make_async_copy(table.at[pl.ds(0, tm), :], buf.at[slot, k], sem.at[slot, k]).wait()
            term = buf[slot, k] * w_ref[:, k:k + 1] if weighted else buf[slot, k]
            acc = term if acc is None else acc + term
        o_ref[...] = acc.astype(o_ref.dtype)

    @pl.when(jnp.logical_not(valid))
    def _():
        o_ref[...] = jnp.zeros(o_ref.shape, o_ref.dtype)


def _gather_call(table, idx, weights, out_dtype, n_rows):
    n_src, t_out = idx.shape
    d = table.shape[1]
    tm = GATHER_TILE
    nt = t_out // tm
    idx3 = idx.reshape(n_src, nt, 1, tm)
    smem = functools.partial(pl.BlockSpec, memory_space=pltpu.SMEM)
    in_specs = ([smem((1, 1, tm), lambda i, nr: (i, 0, 0)) for _ in range(n_src)]
                + [smem((1, 1, tm), lambda i, nr: (jnp.minimum(i + 1, nt - 1), 0, 0)) for _ in range(n_src)]
                + [pl.BlockSpec(memory_space=pl.ANY)])
    args = [idx3[k] for k in range(n_src)] * 2 + [table]
    if weights is not None:
        in_specs.append(pl.BlockSpec((tm, n_src), lambda i, nr: (i, 0)))
        args.append(weights)
    grid_spec = pltpu.PrefetchScalarGridSpec(
        num_scalar_prefetch=1, grid=(nt,), in_specs=in_specs,
        out_specs=pl.BlockSpec((tm, d), lambda i, nr: (i, 0)),
        scratch_shapes=[pltpu.VMEM((2, n_src, tm, d), table.dtype), pltpu.SemaphoreType.DMA((2, n_src))],
    )
    return pl.pallas_call(
        functools.partial(_gather_kernel, n_src=n_src, weighted=weights is not None, tm=tm),
        grid_spec=grid_spec, out_shape=jax.ShapeDtypeStruct((t_out, d), out_dtype),
        compiler_params=_cparams(("arbitrary",)), name="row_gather",
    )(jnp.asarray(n_rows, jnp.int32).reshape(1), *args)


def _moe(h, w_router, w_gate, w_up, w_down):
    t, d = h.shape
    ne = w_router.shape[1]
    tm = FFN_TILE_M
    sel, cw = _router_call(h, w_router)
    hot = (sel > 0).astype(jnp.int32)
    rank = jnp.cumsum(hot, axis=1) - hot
    counts = jnp.sum(hot, axis=1)
    tiles_e = (counts + tm - 1) // tm
    tile_end = jnp.cumsum(tiles_e)
    tile_start = tile_end - tiles_e
    n_valid = tile_end[-1]
    n_tiles = (2 * t + tm - 1) // tm + ne
    tp = n_tiles * tm
    dest = tile_start[:, None] * tm + rank
    tid = jnp.arange(n_tiles, dtype=jnp.int32)
    te = jnp.sum((tid[:, None] >= tile_end[None, :]).astype(jnp.int32), axis=1)
    last = jnp.maximum(n_valid - 1, 0)
    te = jnp.where(tid < n_valid, te, te[last]).astype(jnp.int32)
    xi = jnp.minimum(tid, last).astype(jnp.int32)
    p1 = jnp.sum(jnp.where(sel == 1, dest, 0), axis=0)
    p2 = jnp.sum(jnp.where(sel == 2, dest, 0), axis=0)
    wts = jnp.stack([jnp.sum(jnp.where(sel == 1, cw, 0.0), axis=0),
                     jnp.sum(jnp.where(sel == 2, cw, 0.0), axis=0)], axis=1)

    pidx = jnp.stack([p1, p2]).astype(jnp.int32)
    tok = jnp.arange(t, dtype=jnp.int32)
    src = jnp.zeros((tp,), jnp.int32).at[pidx.reshape(-1)].set(jnp.concatenate([tok, tok]), unique_indices=True)
    xs = _gather_call(h, src[None, :], None, BF16, n_valid * tm)
    ys = _ffn_call(xs, w_gate, w_up, w_down, te, xi, n_valid.reshape(1).astype(jnp.int32), tm)
    return _gather_call(ys, pidx, wts, F32, t)


_SRC_Z_SSD = 0
_SRC_X_SSD = SSD_WIDTH
_SRC_BC = 2 * SSD_WIDTH
_SRC_DT = SSD_WIDTH + SSD_CONV_DIM
_SRC_QKV = _SRC_DT + SSD_HEADS
_SRC_Z_GDN = _SRC_QKV + GDN_CONV_DIM
_SRC_A = _SRC_Z_GDN + GDN_WIDTH
_SRC_B = _SRC_A + GDN_V_HEADS
_SEGMENTS = ((COL_QKV, _SRC_QKV, GDN_CONV_DIM), (COL_Z_SSD, _SRC_Z_SSD, SSD_WIDTH),
             (COL_Z_GDN, _SRC_Z_GDN, GDN_WIDTH), (COL_X_SSD, _SRC_X_SSD, SSD_WIDTH),
             (COL_BC, _SRC_BC, SSD_CONV_DIM - SSD_WIDTH), (COL_SMALL + SMALL_DT, _SRC_DT, SSD_HEADS),
             (COL_SMALL + SMALL_A, _SRC_A, 2 * GDN_V_HEADS))
_USED_COLS = COL_SMALL + SMALL_B + GDN_V_HEADS


def _reorder_kernel(w_ref, o_ref):
    for dst, src, rows in _SEGMENTS:
        o_ref[0, dst:dst + rows, :] = w_ref[0, src:src + rows, :].astype(BF16)
    o_ref[0, _USED_COLS:PROJ_COLS, :] = jnp.zeros((PROJ_COLS - _USED_COLS, o_ref.shape[-1]), BF16)


def _reorder_in_proj(w_in):
    depth, d, n_in = w_in.shape
    tk = 512
    return pl.pallas_call(
        _reorder_kernel, grid=(depth, d // tk),
        in_specs=[pl.BlockSpec((1, n_in, tk), lambda l, k: (l, 0, k))],
        out_specs=pl.BlockSpec((1, PROJ_COLS, tk), lambda l, k: (l, 0, k)),
        out_shape=jax.ShapeDtypeStruct((depth, PROJ_COLS, d), BF16),
        compiler_params=_cparams(("parallel", "parallel")), name="reorder_w_in",
    )(jnp.swapaxes(w_in, 1, 2))


def kernel(x_prompt, x_sample, c_prompt, c_sample, state_ssd_conv, state_ssm, state_gdn_conv, state_gdn, w_ada, b_ada, w_norm_mix, w_norm_ffn, w_in, w_conv_ssd, b_conv_ssd, ssd_dt_bias, ssd_a_log, ssd_d, ssd_norm, w_conv_gdn, gdn_dt_bias, gdn_a_log, gdn_norm, w_out, w_ffn_gate, w_ffn_up, w_ffn_down, w_router, w_exp_gate, w_exp_up, w_exp_down, w_norm_final):
    bp, lp, d = x_prompt.shape
    bs, ls, _ = x_sample.shape
    depth = w_in.shape[0]
    tpr = bp * lp
    tsa = bs * ls
    t = tpr + tsa

    x = (x_prompt.reshape(tpr, d), x_sample.reshape(tsa, d))
    mod = _ada_call(jnp.concatenate([c_sample, c_prompt], axis=0), w_ada, b_ada)
    w_in_t = _reorder_in_proj(w_in)
    w_out_b = w_out.astype(BF16)

    qp = PROMPT_CHUNK if lp % PROMPT_CHUNK == 0 else lp
    nseq_s = 16 // ls if ls < 16 else 1
    sw = GDN_V_HEADS * GDN_HEAD
    zeros_p = (jnp.zeros((1, bp, CONV_WIDTH - 1, SSD_CONV_DIM), F32), jnp.zeros((1, bp, SSD_WIDTH, SSD_STATE), F32),
               jnp.zeros((1, bp, CONV_WIDTH - 1, GDN_CONV_DIM), F32), jnp.zeros((1, bp, sw, GDN_HEAD), F32))
    ssm_in = state_ssm.reshape(depth, bs, SSD_WIDTH, SSD_STATE)
    gdn_in = state_gdn.reshape(depth, bs, sw, GDN_HEAD)

    new_p = [[], [], [], []]
    new_s = [[], [], [], []]
    common = dict(t_prompt=tpr, l_prompt=lp)
    mods = [(mod[l, bs:].reshape(bp, 1, N_MOD * d), jnp.repeat(mod[l, :bs], ls, axis=0)) for l in range(depth)]
    h = _ew_call(x, w_norm_mix[0], mod=(1, 0) + mods[0], **common)
    for l in range(depth):
        mod_p, mod_s = mods[l]
        lw = {"w_conv_ssd": w_conv_ssd[l], "b_conv_ssd": b_conv_ssd[l], "ssd_dt_bias": ssd_dt_bias[l],
              "ssd_a_log": ssd_a_log[l], "ssd_d": ssd_d[l], "ssd_norm": ssd_norm[l],
              "w_conv_gdn": w_conv_gdn[l], "gdn_dt_bias": gdn_dt_bias[l], "gdn_a_log": gdn_a_log[l],
              "gdn_norm": gdn_norm[l]}
        proj = _in_proj_call(h, w_in_t, l)

        prompt = dict(row0=0, bsz=bp, seq=lp, q=qp, nseq=1)
        sample = dict(row0=tpr, bsz=bs, seq=ls, q=ls, nseq=nseq_s)
        yp_ssd, pc, pst = _ssd_call(proj, lw, zeros_p[0], zeros_p[1], 0, **prompt)
        ys_ssd, sc_, sst = _ssd_call(proj, lw, state_ssd_conv, ssm_in, l, **sample)
        yp_gdn, pgc, pgst = _gdn_call(proj, lw, zeros_p[2], zeros_p[3], 0, **prompt)
        ys_gdn, sgc, sgst = _gdn_call(proj, lw, state_gdn_conv, gdn_in, l, **sample)
        for lst, v in zip(new_p, (pc, pst, pgc, pgst)):
            lst.append(v)
        for lst, v in zip(new_s, (sc_, sst, sgc, sgst)):
            lst.append(v)

        mix = _out_proj_call(yp_ssd, yp_gdn, ys_ssd, ys_gdn, w_out_b[l])
        moe_layer = l % 2 == 1
        x, h2 = _ew_call(x, w_norm_ffn[l], res=(mix, 2, mod_p, mod_s), mod=(4, 3, mod_p, mod_s), out_x=True,
                         h_dtype=F32 if moe_layer else BF16, **common)
        i = l // 2
        if moe_layer:
            f = _moe(h2, w_router[i], w_exp_gate[i], w_exp_up[i], w_exp_down[i])
        else:
            nt = t // FFN_TILE_M
            tid = jnp.arange(nt, dtype=jnp.int32)
            f = _ffn_call(h2, w_ffn_gate[i][None], w_ffn_up[i][None], w_ffn_down[i][None],
                          jnp.zeros((nt,), jnp.int32), tid, jnp.full((1,), nt, jnp.int32), FFN_TILE_M)
        if l + 1 < depth:
            x, h = _ew_call(x, w_norm_mix[l + 1], res=(f, 5, mod_p, mod_s), mod=(1, 0) + mods[l + 1],
                            out_x=True, **common)
        else:
            y_p, y_s = _ew_call(x, w_norm_final, res=(f, 5, mod_p, mod_s), h_dtype=F32, split_out=True, **common)

    def stack_states(lst, bsz):
        conv_s = jnp.stack(lst[0])
        ssm = jnp.stack(lst[1]).reshape(depth, bsz, SSD_HEADS, SSD_HEAD_DIM, SSD_STATE)
        conv_g = jnp.stack(lst[2])
        gdn = jnp.stack(lst[3]).reshape(depth, bsz, GDN_V_HEADS, GDN_HEAD, GDN_HEAD)
        return conv_s, ssm, conv_g, gdn

    return ((y_p.reshape(bp, lp, d), y_s.reshape(bs, ls, d))
            + stack_states(new_p, bp) + stack_states(new_s, bs))
```

```python
import functools

import jax
import jax.numpy as jnp
from jax import lax
from jax.experimental import pallas as pl
from jax.experimental.pallas import tpu as pltpu

F32 = jnp.float32
BF16 = jnp.bfloat16

D_MODEL = 2048
CONV_WIDTH = 4
SSD_WIDTH = 1024
SSD_HEAD_DIM = 64
SSD_HEADS = 16
SSD_GROUPS = 2
SSD_STATE = 128
SSD_CONV_DIM = SSD_WIDTH + 2 * SSD_GROUPS * SSD_STATE
GDN_WIDTH = 1024
GDN_HEAD = 128
GDN_V_HEADS = 8
GDN_K_HEADS = 4
GDN_QK_DIM = GDN_K_HEADS * GDN_HEAD
GDN_CONV_DIM = 2 * GDN_QK_DIM + GDN_WIDTH
N_EXPERTS = 8
N_MOD = 6
EPS = 1e-6

PROJ_COLS = 5760
COL_QKV, COL_Z_SSD, COL_Z_GDN, COL_X_SSD, COL_BC, COL_SMALL = 0, 2048, 3072, 4096, 5120, 5632
SMALL_DT, SMALL_A, SMALL_B = 0, SSD_HEADS, SSD_HEADS + GDN_V_HEADS

V7X_VMEM_LIMIT = 56 * 1024 * 1024
TOKEN_TILE = 256
MM_TILE_M = 1024
IN_PROJ_TILE_N = 1920
FFN_TILE_M = 768
MOE_TILE_M = 800
MOE_GATHER_TILE = 400
FFN_TILE_F = 512
GATHER_TILE = 256
GATHER_UNROLL = 8
PROMPT_CHUNK = 128
NEG_BIG = -1e30


def _cparams(sem):
    return pltpu.CompilerParams(dimension_semantics=sem, vmem_limit_bytes=V7X_VMEM_LIMIT)


_NN = (((1,), (0,)), ((), ()))
_NT = (((1,), (1,)), ((), ()))
_TN = (((0,), (0,)), ((), ()))


def _dg(a, b, dims):
    return lax.dot_general(a, b, dims, preferred_element_type=F32)


def _dot(a, b, dims=_NN):
    return _dg(a.astype(BF16), b.astype(BF16), dims)


def _split(x, n):
    parts = []
    r = x
    for _ in range(n - 1):
        p = r.astype(BF16)
        parts.append(p)
        r = r - p.astype(F32)
    parts.append(r.astype(BF16))
    return parts


def _dot_exact_rhs(a, b, dims=_NN):
    bb = b.astype(BF16)
    a1, a2, a3 = _split(a, 3)
    return _dg(a3, bb, dims) + _dg(a2, bb, dims) + _dg(a1, bb, dims)


def _dot_exact_lhs(a, b, dims=_NN):
    ab = a.astype(BF16)
    b1, b2, b3 = _split(b, 3)
    return _dg(ab, b3, dims) + _dg(ab, b2, dims) + _dg(ab, b1, dims)


def _dot6(a, b, dims=_NN):
    a1, a2, a3 = _split(a, 3)
    b1, b2, b3 = _split(b, 3)
    small = _dg(a1, b3, dims) + _dg(a2, b2, dims) + _dg(a3, b1, dims)
    mid = _dg(a1, b2, dims) + _dg(a2, b1, dims)
    return small + mid + _dg(a1, b1, dims)


def _sigmoid(x):
    return 1.0 / (1.0 + jnp.exp(-x))


def _silu(x):
    return x * _sigmoid(x)


def _softplus(x):
    return jnp.maximum(x, 0.0) + jnp.log1p(jnp.exp(-jnp.abs(x)))


def _iota(shape, dim):
    return lax.broadcasted_iota(jnp.int32, shape, dim)


def _transpose_cols(x, n):
    eye = (_iota((n, n), 0) == _iota((n, n), 1)).astype(F32)
    return _dot_exact_lhs(eye, x, _NT)


def _conv_silu(u, ext_ref, s, w_ref, bias, q):
    ext_ref[s, 8:8 + q, :] = u
    acc = u * w_ref[CONV_WIDTH - 1:CONV_WIDTH, :]
    if bias is not None:
        acc = acc + bias
    for j in range(1, CONV_WIDTH):
        acc = acc + ext_ref[s, 8 - j:8 - j + q, :] * w_ref[CONV_WIDTH - 1 - j:CONV_WIDTH - j, :]
    return _silu(acc)


def _conv_advance(ext_ref, s, q):
    ext_ref[s, 0:8, :] = ext_ref[s, q:q + 8, :]


def _conv_init(ext_ref, s, state_rows):
    ext_ref[s, 0:8, :] = jnp.zeros((8, ext_ref.shape[-1]), F32)
    ext_ref[s, 8 - (CONV_WIDTH - 1):8, :] = state_rows


def _conv_tail(ext_ref, s, q):
    return ext_ref[s, q + 8 - (CONV_WIDTH - 1):q + 8, :]


def _ada_kernel(c_ref, w_ref, b_ref, o_ref):
    a = _silu(c_ref[...]).astype(BF16)
    o_ref[0] = jnp.dot(a, w_ref[0].astype(BF16), preferred_element_type=F32) + b_ref[0]


def _ada_call(c_all, w_ada, b_ada):
    depth, d, n = w_ada.shape
    m = c_all.shape[0]
    tn = 1024
    return pl.pallas_call(
        _ada_kernel,
        grid=(depth, n // tn),
        in_specs=[
            pl.BlockSpec((m, d), lambda l, j: (0, 0)),
            pl.BlockSpec((1, d, tn), lambda l, j: (l, 0, j)),
            pl.BlockSpec((1, 1, tn), lambda l, j: (l, 0, j)),
        ],
        out_specs=pl.BlockSpec((1, m, tn), lambda l, j: (l, 0, j)),
        out_shape=jax.ShapeDtypeStruct((depth, m, n), F32),
        compiler_params=_cparams(("parallel", "parallel")),
        name="ada_mod",
    )(c_all, w_ada, b_ada.reshape(depth, 1, n))


def _ew_kernel(*refs, split_in, has_res, modded, out_x, split_out, n_prompt_tiles):
    refs = list(refs)
    xp_ref = refs.pop(0)
    xs_ref = refs.pop(0) if split_in else xp_ref
    if has_res:
        y_ref, gp_ref, gs_ref = refs.pop(0), refs.pop(0), refs.pop(0)
    w_ref = refs.pop(0)
    if modded:
        scp_ref, shp_ref, scs_ref, shs_ref = refs.pop(0), refs.pop(0), refs.pop(0), refs.pop(0)
    if out_x:
        xo_ref = refs.pop(0)
    hp_ref = refs.pop(0)
    hs_ref = refs.pop(0) if split_out else hp_ref
    i = pl.program_id(0)

    def run(prompt):
        x = (xp_ref if prompt else xs_ref)[...]
        if has_res:
            g = gp_ref[0] if prompt else gs_ref[...]
            x = x + g * y_ref[...]
            if out_x:
                xo_ref[...] = x
        h = x * lax.rsqrt(jnp.mean(x * x, axis=-1, keepdims=True) + EPS) * w_ref[...]
        if modded:
            sc = scp_ref[0] if prompt else scs_ref[...]
            sh = shp_ref[0] if prompt else shs_ref[...]
            h = h * (1.0 + sc) + sh
        h_ref = hp_ref if prompt else hs_ref
        h_ref[...] = h.astype(h_ref.dtype)

    @pl.when(i < n_prompt_tiles)
    def _():
        run(True)

    @pl.when(i >= n_prompt_tiles)
    def _():
        run(False)


def _ew_call(x, w, *, t_prompt, l_prompt, res=None, mod=None, out_x=False, h_dtype=BF16, split_out=False):
    split_in = isinstance(x, tuple)
    d = w.shape[0]
    t = sum(a.shape[0] for a in x) if split_in else x.shape[0]
    tm = TOKEN_TILE
    npt = t_prompt // tm
    per_seq = l_prompt // tm

    def pspec(chunk, mp):
        bp = mp.shape[0]
        return pl.BlockSpec((1, 1, d), lambda i: (jnp.minimum(i // per_seq, bp - 1), 0, chunk))

    def sspec(chunk):
        return pl.BlockSpec((tm, d), lambda i: (jnp.maximum(i - npt, 0), chunk))

    row = pl.BlockSpec((tm, d), lambda i: (i, 0))
    prow = pl.BlockSpec((tm, d), lambda i: (jnp.minimum(i, npt - 1), 0))
    srow = sspec(0)
    args, specs = (list(x), [prow, srow]) if split_in else ([x], [row])
    if res is not None:
        y, gch, gmp, gms = res
        args += [y, gmp, gms]
        specs += [row, pspec(gch, gmp), sspec(gch)]
    args.append(w.reshape(1, d))
    specs.append(pl.BlockSpec((1, d), lambda i: (0, 0)))
    if mod is not None:
        scc, shc, mmp, mms = mod
        args += [mmp, mmp, mms, mms]
        specs += [pspec(scc, mmp), pspec(shc, mmp), sspec(scc), sspec(shc)]
    out_shape, out_specs = [], []
    if out_x:
        out_shape.append(jax.ShapeDtypeStruct((t, d), F32))
        out_specs.append(row)
    if split_out:
        out_shape += [jax.ShapeDtypeStruct((t_prompt, d), h_dtype), jax.ShapeDtypeStruct((t - t_prompt, d), h_dtype)]
        out_specs += [prow, srow]
    else:
        out_shape.append(jax.ShapeDtypeStruct((t, d), h_dtype))
        out_specs.append(row)
    kern = functools.partial(_ew_kernel, split_in=split_in, has_res=res is not None, modded=mod is not None,
                             out_x=out_x, split_out=split_out, n_prompt_tiles=npt)
    outs = pl.pallas_call(
        kern, grid=(t // tm,), in_specs=specs, out_specs=out_specs, out_shape=out_shape,
        compiler_params=_cparams(("arbitrary",)), name="token_norm",
    )(*args)
    return outs if len(outs) > 1 else outs[0]


def _in_proj_kernel(a_ref, wt_ref, o_ref):
    o_ref[...] = _dg(a_ref[...], wt_ref[...], _NT)


def _in_proj_call(a, w_t, layer):
    m, k = a.shape
    n = w_t.shape[1]
    tn = IN_PROJ_TILE_N
    tm = MM_TILE_M
    while m % tm:
        tm //= 2
    return pl.pallas_call(
        _in_proj_kernel,
        grid=(m // tm, n // tn),
        in_specs=[pl.BlockSpec((tm, k), lambda i, j: (i, 0)),
                  pl.BlockSpec((None, tn, k), lambda i, j: (layer, j, 0))],
        out_specs=pl.BlockSpec((tm, tn), lambda i, j: (i, j)),
        out_shape=jax.ShapeDtypeStruct((m, n), F32),
        compiler_params=_cparams(("parallel", "parallel")),
        name="in_proj",
    )(a, w_t)


def _out_proj_kernel(ap1_ref, ap2_ref, as1_ref, as2_ref, w1_ref, w2_ref, o_ref, *, n_prompt_tiles):
    i = pl.program_id(0)

    def run(a1_ref, a2_ref):
        o_ref[...] = (jnp.dot(a1_ref[...], w1_ref[...], preferred_element_type=F32)
                      + jnp.dot(a2_ref[...], w2_ref[...], preferred_element_type=F32))

    @pl.when(i < n_prompt_tiles)
    def _():
        run(ap1_ref, ap2_ref)

    @pl.when(i >= n_prompt_tiles)
    def _():
        run(as1_ref, as2_ref)


def _out_proj_call(yp_ssd, yp_gdn, ys_ssd, ys_gdn, w_out):
    tp, k1 = yp_ssd.shape
    ts = ys_ssd.shape[0]
    k2 = yp_gdn.shape[1]
    n = w_out.shape[1]
    tm = MM_TILE_M
    while tp % tm or ts % tm:
        tm //= 2
    tn = 1024
    npt = tp // tm
    pmap = lambda i, j: (jnp.minimum(i, npt - 1), 0)
    smap = lambda i, j: (jnp.maximum(i - npt, 0), 0)
    return pl.pallas_call(
        functools.partial(_out_proj_kernel, n_prompt_tiles=npt),
        grid=((tp + ts) // tm, n // tn),
        in_specs=[pl.BlockSpec((tm, k1), pmap), pl.BlockSpec((tm, k2), pmap),
                  pl.BlockSpec((tm, k1), smap), pl.BlockSpec((tm, k2), smap),
                  pl.BlockSpec((k1, tn), lambda i, j: (0, j)),
                  pl.BlockSpec((k2, tn), lambda i, j: (k1 // k2, j))],
        out_specs=pl.BlockSpec((tm, tn), lambda i, j: (i, j)),
        out_shape=jax.ShapeDtypeStruct((tp + ts, n), F32),
        compiler_params=_cparams(("parallel", "parallel")),
        name="out_proj",
    )(yp_ssd, yp_gdn, ys_ssd, ys_gdn, w_out, w_out)


def _store_state(sto_ref, s, value, stack):
    if stack == "first":
        sto_ref[0, s] = value
        for l in range(1, sto_ref.shape[0]):
            sto_ref[l, s] = jnp.zeros(value.shape, value.dtype)
    else:
        sto_ref[s] = value


def _ssd_kernel(*refs, q, nseq, nc, stack):
    (z_ref, xs_ref, bc_ref, sm_ref, wx_ref, wbc_ref, bx_ref, bbc_ref, dtb_ref, alog_ref,
     dexp_ref, nw_ref, cst_ref, st_ref) = refs[:14]
    y_ref, cso_ref, sto_ref, extx, extbc, s_scr = refs[14 + (stack == "next"):]
    c = pl.program_id(1)
    nh = SSD_HEADS
    hg = SSD_HEADS // SSD_GROUPS
    wg = SSD_WIDTH // SSD_GROUPS
    p = SSD_HEAD_DIM
    n = SSD_STATE
    gn = SSD_GROUPS * n

    li = _iota((q, q), 0)
    si = _iota((q, q), 1)
    tril = si <= li
    tri = tril.astype(F32)
    expand = (_iota((nh, SSD_WIDTH), 1) // p == _iota((nh, SSD_WIDTH), 0)).astype(F32)
    expand_t = (_iota((SSD_WIDTH, nh), 0) // p == _iota((SSD_WIDTH, nh), 1)).astype(F32)
    lane = _iota((q, 128), 1)

    for s in range(nseq):
        rows = slice(s * q, (s + 1) * q)

        @pl.when(c == 0)
        def _():
            _conv_init(extx, s, cst_ref[s, :, 0:SSD_WIDTH])
            _conv_init(extbc, s, cst_ref[s, :, SSD_WIDTH:SSD_CONV_DIM])
            s_scr[s] = st_ref[s]

        xs = _conv_silu(xs_ref[rows, :], extx, s, wx_ref, bx_ref[...], q)
        bcv = _conv_silu(bc_ref[rows, :], extbc, s, wbc_ref, bbc_ref[...], q)

        @pl.when(c == nc - 1)
        def _():
            cso_ref[s, :, 0:SSD_WIDTH] = _conv_tail(extx, s, q)
            cso_ref[s, :, SSD_WIDTH:SSD_CONV_DIM] = _conv_tail(extbc, s, q)

        _conv_advance(extx, s, q)
        _conv_advance(extbc, s, q)

        dt = _softplus(sm_ref[rows, :][:, SMALL_DT:SMALL_DT + nh] + dtb_ref[...])
        a = dt * (-jnp.exp(alog_ref[...]))
        a_cum = _dot_exact_lhs(tri, a)
        a_cum_t = _transpose_cols(a_cum, nh)
        dt_e = _dot_exact_rhs(dt, expand)
        acum_e = _dot_exact_rhs(a_cum, expand)
        xdt = xs * dt_e
        alast_e = acum_e[q - 1:q, :]
        xd = xdt * jnp.exp(alast_e - acum_e)
        alast_b = jnp.broadcast_to(a_cum_t[:, q - 1:q], (nh, 128))
        dec_rows = jnp.exp(_dot_exact_lhs(expand_t, alast_b))

        pieces = []
        for g in range(SSD_GROUPS):
            bm = bcv[:, g * n:(g + 1) * n]
            cm = bcv[:, gn + g * n:gn + (g + 1) * n]
            cb = _dot(cm, bm, _NT)
            s_old = s_scr[s, g * wg:(g + 1) * wg, :]
            pieces.append(_dot(cm, s_old, _NT))
            s_scr[s, g * wg:(g + 1) * wg, :] = (dec_rows[g * wg:(g + 1) * wg, :] * s_old
                                                + _dot(xd[:, g * wg:(g + 1) * wg], bm, _TN))
            for k in range(hg // 2):
                ms = []
                for hh in (g * hg + 2 * k, g * hg + 2 * k + 1):
                    seg = a_cum[:, hh:hh + 1] - a_cum_t[hh:hh + 1, :]
                    ms.append(cb * jnp.exp(jnp.where(tril, seg, NEG_BIG)))
                c0 = g * wg + 2 * p * k
                xp = xdt[:, c0:c0 + 2 * p]
                top = jnp.where(lane < p, xp, 0.0)
                bot = jnp.where(lane >= p, xp, 0.0)
                if q % 128 == 0:
                    piece = _dot(jnp.concatenate(ms, axis=1), jnp.concatenate([top, bot], axis=0))
                else:
                    piece = _dot(ms[0], top) + _dot(ms[1], bot)
                pieces.append(piece)
        y_off = jnp.concatenate([pieces[0], pieces[1 + hg // 2]], axis=1) * jnp.exp(acum_e)
        y_diag = jnp.concatenate(pieces[1:1 + hg // 2] + pieces[2 + hg // 2:], axis=1)
        y = y_off + y_diag + dexp_ref[...] * xs
        y = y * _silu(z_ref[rows, :])
        outs = []
        for g in range(SSD_GROUPS):
            yg = y[:, g * wg:(g + 1) * wg]
            outs.append(yg * lax.rsqrt(jnp.mean(yg * yg, axis=-1, keepdims=True) + EPS))
        y_ref[rows, :] = (jnp.concatenate(outs, axis=1) * nw_ref[...]).astype(y_ref.dtype)

        @pl.when(c == nc - 1)
        def _():
            _store_state(sto_ref, s, s_scr[s], stack)


def _state_out(stacked, layer, bsz, nseq, rows, n, in_specs, args):
    if stacked is None:
        return (jax.ShapeDtypeStruct((bsz, rows, n), F32),
                pl.BlockSpec((nseq, rows, n), lambda b, c: (b, 0, 0)), None, {})
    depth, prev = stacked
    shape = jax.ShapeDtypeStruct((depth, bsz, rows, n), F32)
    if prev is None:
        return shape, pl.BlockSpec((depth, nseq, rows, n), lambda b, c: (0, b, 0, 0)), "first", {}
    in_specs.append(pl.BlockSpec(memory_space=pl.ANY))
    args.append(prev)
    return (shape, pl.BlockSpec((None, nseq, rows, n), lambda b, c: (layer, b, 0, 0)), "next",
            {len(args) - 1: 2})


def _ssd_call(proj, lw, conv_state, ssm_state, layer, *, row0, bsz, seq, q, nseq, stacked=None):
    nc = seq // q
    r = nseq * q
    assert nseq == 1 or nc == 1
    rb0 = row0 // r
    n = SSD_STATE
    bcw = SSD_CONV_DIM - SSD_WIDTH
    grid = (bsz // nseq, nc)
    rowi = lambda b, c: rb0 + b * nc + c
    full = lambda shape: pl.BlockSpec(shape, lambda b, c: (0,) * len(shape))

    in_specs = [
        pl.BlockSpec((r, SSD_WIDTH), lambda b, c: (rowi(b, c), COL_Z_SSD // SSD_WIDTH)),
        pl.BlockSpec((r, SSD_WIDTH), lambda b, c: (rowi(b, c), COL_X_SSD // SSD_WIDTH)),
        pl.BlockSpec((r, bcw), lambda b, c: (rowi(b, c), COL_BC // bcw)),
        pl.BlockSpec((r, 128), lambda b, c: (rowi(b, c), COL_SMALL // 128)),
        pl.BlockSpec((CONV_WIDTH, SSD_WIDTH), lambda b, c: (0, 0)),
        pl.BlockSpec((CONV_WIDTH, bcw), lambda b, c: (0, SSD_WIDTH // bcw)),
        pl.BlockSpec((1, SSD_WIDTH), lambda b, c: (0, 0)),
        pl.BlockSpec((1, bcw), lambda b, c: (0, SSD_WIDTH // bcw)),
        full((1, SSD_HEADS)), full((1, SSD_HEADS)), full((1, SSD_WIDTH)), full((1, SSD_WIDTH)),
        pl.BlockSpec((None, nseq, CONV_WIDTH - 1, SSD_CONV_DIM), lambda b, c: (layer, b, 0, 0)),
        pl.BlockSpec((None, nseq, SSD_WIDTH, n), lambda b, c: (layer, b, 0, 0)),
    ]
    cw = lw["w_conv_ssd"]
    cbias = lw["b_conv_ssd"].reshape(1, SSD_CONV_DIM)
    args = [proj, proj, proj, proj, cw, cw, cbias, cbias,
            lw["ssd_dt_bias"].reshape(1, SSD_HEADS), lw["ssd_a_log"].reshape(1, SSD_HEADS),
            jnp.repeat(lw["ssd_d"], SSD_HEAD_DIM).reshape(1, SSD_WIDTH), lw["ssd_norm"].reshape(1, SSD_WIDTH),
            conv_state, ssm_state]
    st_shape, st_spec, stack, aliases = _state_out(stacked, layer, bsz, nseq, SSD_WIDTH, n, in_specs, args)
    out_shape = [
        jax.ShapeDtypeStruct((bsz * seq, SSD_WIDTH), BF16),
        jax.ShapeDtypeStruct((bsz, CONV_WIDTH - 1, SSD_CONV_DIM), F32),
        st_shape,
    ]
    out_specs = [
        pl.BlockSpec((r, SSD_WIDTH), lambda b, c: (b * nc + c, 0)),
        pl.BlockSpec((nseq, CONV_WIDTH - 1, SSD_CONV_DIM), lambda b, c: (b, 0, 0)),
        st_spec,
    ]
    scratch = [pltpu.VMEM((nseq, q + 8, SSD_WIDTH), F32), pltpu.VMEM((nseq, q + 8, bcw), F32),
               pltpu.VMEM((nseq, SSD_WIDTH, n), F32)]
    kern = functools.partial(_ssd_kernel, q=q, nseq=nseq, nc=nc, stack=stack)
    return pl.pallas_call(
        kern, grid=grid, in_specs=in_specs, out_specs=out_specs, out_shape=out_shape,
        scratch_shapes=scratch, input_output_aliases=aliases,
        compiler_params=_cparams(("parallel", "arbitrary")), name="ssd_mixer",
    )(*args)


def _neumann_inverse(xs, nil):
    n = xs[0].shape[0]
    eye = (_iota((n, n), 0) == _iota((n, n), 1)).astype(F32)
    invs = [eye - x for x in xs]
    pws = xs
    k = 2
    while k < nil:
        pws = [_dot(pw, pw) for pw in pws]
        invs = [inv + _dot(inv, pw) for inv, pw in zip(invs, pws)]
        k *= 2
    return invs


def _unit_lower_inverse(ms, q):
    blk = 16
    if q <= blk:
        return _neumann_inverse(ms, q)
    same = (_iota((q, q), 0) // blk) == (_iota((q, q), 1) // blk)
    dgs = [jnp.where(same, m, 0.0) for m in ms]
    dinvs = _neumann_inverse(dgs, blk)
    nns = [_dot(dinv, m - dg) for dinv, m, dg in zip(dinvs, ms, dgs)]
    ninvs = _neumann_inverse(nns, q // blk)
    return [_dot(ninv, dinv) for ninv, dinv in zip(ninvs, dinvs)]


def _gdn_kernel(*refs, q, nseq, nc, stack):
    qkv_ref, z_ref, sm_ref, w_ref, dtb_ref, alog_ref, nw_ref, cst_ref, st_ref = refs[:9]
    y_ref, cso_ref, sto_ref, ext, s_scr = refs[9 + (stack == "next"):]
    c = pl.program_id(1)
    nv = GDN_V_HEADS
    rep = GDN_V_HEADS // GDN_K_HEADS
    dk = GDN_HEAD
    dv = GDN_HEAD

    li = _iota((q, q), 0)
    si = _iota((q, q), 1)
    tril = si <= li
    strict = si < li
    tri = tril.astype(F32)

    heads = []
    for s in range(nseq):
        rows = slice(s * q, (s + 1) * q)

        @pl.when(c == 0)
        def _():
            _conv_init(ext, s, cst_ref[s])
            s_scr[s] = st_ref[s]

        xc = _conv_silu(qkv_ref[rows, :], ext, s, w_ref, None, q)

        @pl.when(c == nc - 1)
        def _():
            cso_ref[s] = _conv_tail(ext, s, q)

        _conv_advance(ext, s, q)

        sm = sm_ref[rows, :]
        a_raw = sm[:, SMALL_A:SMALL_A + nv]
        b_raw = sm[:, SMALL_B:SMALL_B + nv]
        gate = -jnp.exp(alog_ref[...]) * _softplus(a_raw + dtb_ref[...])
        g_cum = _dot_exact_lhs(tri, gate)
        g_cum_t = _transpose_cols(g_cum, nv)
        beta_all = _sigmoid(b_raw)

        for kh in range(GDN_K_HEADS):
            qc = xc[:, kh * dk:(kh + 1) * dk]
            kc = xc[:, GDN_QK_DIM + kh * dk:GDN_QK_DIM + (kh + 1) * dk]
            qn = qc * lax.rsqrt(jnp.sum(qc * qc, axis=-1, keepdims=True) + EPS) * (dk ** -0.5)
            kn = kc * lax.rsqrt(jnp.sum(kc * kc, axis=-1, keepdims=True) + EPS)
            kk = _dot(kn, kn, _NT)
            qk = _dot(qn, kn, _NT)
            for j in range(kh * rep, (kh + 1) * rep):
                col = g_cum[:, j:j + 1]
                seg = col - g_cum_t[j:j + 1, :]
                decay = jnp.exp(jnp.where(tril, seg, NEG_BIG))
                beta = beta_all[:, j:j + 1]
                e_col = jnp.exp(col)
                g_last = g_cum[q - 1:q, j:j + 1]
                vj = xc[:, 2 * GDN_QK_DIM + j * dv:2 * GDN_QK_DIM + (j + 1) * dv]
                heads.append(dict(
                    s=s, j=j, rows=rows,
                    m=jnp.where(strict, beta * kk * decay, 0.0),
                    rhs=jnp.concatenate([vj * beta, kn * (beta * e_col)], axis=1),
                    attn=qk * decay, q_dec=qn * e_col, k_dec=kn * jnp.exp(g_last - col),
                    s_dec=jnp.exp(g_last)))

    t_invs = _unit_lower_inverse([hd["m"] for hd in heads], q)
    sols = [_dot(t_inv, hd["rhs"]) for t_inv, hd in zip(t_invs, heads)]
    states = [s_scr[hd["s"], dk * hd["j"]:dk * (hd["j"] + 1), :] for hd in heads]
    boths = [_dot(jnp.concatenate([sol[:, dv:], hd["q_dec"]], axis=0), st)
             for sol, hd, st in zip(sols, heads, states)]
    v_news = [sol[:, :dv] - both[:q] for sol, both in zip(sols, boths)]
    outs = [both[q:] + _dot(hd["attn"], v_new) for both, hd, v_new in zip(boths, heads, v_news)]
    for hd, st, v_new in zip(heads, states, v_news):
        s_scr[hd["s"], dk * hd["j"]:dk * (hd["j"] + 1), :] = st * hd["s_dec"] + _dot(hd["k_dec"], v_new, _TN)
    for s in range(nseq):
        rows = slice(s * q, (s + 1) * q)
        ys = []
        for hd, o in zip(heads, outs):
            if hd["s"] == s:
                o = o * lax.rsqrt(jnp.mean(o * o, axis=-1, keepdims=True) + EPS) * nw_ref[...]
                ys.append(o * _silu(z_ref[rows, dv * hd["j"]:dv * (hd["j"] + 1)]))
        y_ref[rows, :] = jnp.concatenate(ys, axis=1).astype(y_ref.dtype)

        @pl.when(c == nc - 1)
        def _():
            _store_state(sto_ref, s, s_scr[s], stack)


def _gdn_call(proj, lw, conv_state, gdn_state, layer, *, row0, bsz, seq, q, nseq, stacked=None):
    nc = seq // q
    r = nseq * q
    assert nseq == 1 or nc == 1
    rb0 = row0 // r
    hd = GDN_HEAD
    sw = GDN_V_HEADS * hd
    grid = (bsz // nseq, nc)
    rowi = lambda b, c: rb0 + b * nc + c
    full = lambda shape: pl.BlockSpec(shape, lambda b, c: (0,) * len(shape))

    in_specs = [
        pl.BlockSpec((r, GDN_CONV_DIM), lambda b, c: (rowi(b, c), COL_QKV // GDN_CONV_DIM)),
        pl.BlockSpec((r, GDN_WIDTH), lambda b, c: (rowi(b, c), COL_Z_GDN // GDN_WIDTH)),
        pl.BlockSpec((r, 128), lambda b, c: (rowi(b, c), COL_SMALL // 128)),
        full((CONV_WIDTH, GDN_CONV_DIM)), full((1, GDN_V_HEADS)), full((1, GDN_V_HEADS)), full((1, hd)),
        pl.BlockSpec((None, nseq, CONV_WIDTH - 1, GDN_CONV_DIM), lambda b, c: (layer, b, 0, 0)),
        pl.BlockSpec((None, nseq, sw, hd), lambda b, c: (layer, b, 0, 0)),
    ]
    args = [proj, proj, proj, lw["w_conv_gdn"], lw["gdn_dt_bias"].reshape(1, GDN_V_HEADS),
            lw["gdn_a_log"].reshape(1, GDN_V_HEADS), lw["gdn_norm"].reshape(1, hd), conv_state, gdn_state]
    st_shape, st_spec, stack, aliases = _state_out(stacked, layer, bsz, nseq, sw, hd, in_specs, args)
    out_shape = [
        jax.ShapeDtypeStruct((bsz * seq, GDN_WIDTH), BF16),
        jax.ShapeDtypeStruct((bsz, CONV_WIDTH - 1, GDN_CONV_DIM), F32),
        st_shape,
    ]
    out_specs = [
        pl.BlockSpec((r, GDN_WIDTH), lambda b, c: (b * nc + c, 0)),
        pl.BlockSpec((nseq, CONV_WIDTH - 1, GDN_CONV_DIM), lambda b, c: (b, 0, 0)),
        st_spec,
    ]
    scratch = [pltpu.VMEM((nseq, q + 8, GDN_CONV_DIM), F32), pltpu.VMEM((nseq, sw, hd), F32)]
    kern = functools.partial(_gdn_kernel, q=q, nseq=nseq, nc=nc, stack=stack)
    return pl.pallas_call(
        kern, grid=grid, in_specs=in_specs, out_specs=out_specs, out_shape=out_shape,
        scratch_shapes=scratch, input_output_aliases=aliases,
        compiler_params=_cparams(("parallel", "arbitrary")), name="gdn_mixer",
    )(*args)


def _ffn_kernel(te_ref, xi_ref, nv_ref, x_ref, wg_ref, wu_ref, wd_ref, o_ref, *scratch):
    i = pl.program_id(0)
    j = pl.program_id(1)
    valid = i < nv_ref[0]

    @pl.when(j == 0)
    def _():
        o_ref[...] = jnp.zeros(o_ref.shape, o_ref.dtype)
        if scratch:
            scratch[0][...] = x_ref[...].astype(BF16)

    @pl.when(valid)
    def _():
        x = scratch[0][...] if scratch else x_ref[...]
        gate = jnp.dot(x, wg_ref[0].astype(BF16), preferred_element_type=F32)
        up = jnp.dot(x, wu_ref[0].astype(BF16), preferred_element_type=F32)
        act = (_silu(gate) * up).astype(BF16)
        o_ref[...] += jnp.dot(act, wd_ref[0].astype(BF16), preferred_element_type=F32)


def _ffn_call(x, w_gate, w_up, w_down, tile_expert, tile_src, n_valid, tm):
    tp, d = x.shape
    f = w_gate.shape[2]
    tf = FFN_TILE_F
    nf = f // tf
    nt = tp // tm

    def jeff(i, j, nv):
        return jnp.where(i < nv[0], j, nf - 1)

    grid_spec = pltpu.PrefetchScalarGridSpec(
        num_scalar_prefetch=3,
        grid=(nt, nf),
        in_specs=[
            pl.BlockSpec((tm, d), lambda i, j, te, xi, nv: (xi[i], 0)),
            pl.BlockSpec((1, d, tf), lambda i, j, te, xi, nv: (te[i], 0, jeff(i, j, nv))),
            pl.BlockSpec((1, d, tf), lambda i, j, te, xi, nv: (te[i], 0, jeff(i, j, nv))),
            pl.BlockSpec((1, tf, d), lambda i, j, te, xi, nv: (te[i], jeff(i, j, nv), 0)),
        ],
        out_specs=pl.BlockSpec((tm, d), lambda i, j, te, xi, nv: (i, 0)),
        scratch_shapes=[] if x.dtype == BF16 else [pltpu.VMEM((tm, d), BF16)],
    )
    return pl.pallas_call(
        _ffn_kernel, grid_spec=grid_spec, out_shape=jax.ShapeDtypeStruct((tp, d), F32),
        compiler_params=_cparams(("arbitrary", "arbitrary")), name="grouped_swiglu",
    )(tile_expert, tile_src, n_valid, x, w_gate, w_up, w_down)


def _router_kernel(h_ref, wr_ref, sel_ref, cw_ref):
    logits = _dot6(wr_ref[...], h_ref[...], _NT)
    ne = logits.shape[0]
    ei = _iota(logits.shape, 0)
    m1 = jnp.max(logits, axis=0, keepdims=True)
    i1 = jnp.min(jnp.where(logits == m1, ei, ne), axis=0, keepdims=True)
    rest = jnp.where(ei == i1, -jnp.inf, logits)
    m2 = jnp.max(rest, axis=0, keepdims=True)
    i2 = jnp.min(jnp.where(rest == m2, ei, ne), axis=0, keepdims=True)
    e = jnp.exp(m2 - m1)
    w1 = 1.0 / (1.0 + e)
    w2 = e / (1.0 + e)
    sel_ref[...] = jnp.where(ei == i1, 1, jnp.where(ei == i2, 2, 0)).astype(jnp.int32)
    cw_ref[...] = jnp.where(ei == i1, w1, jnp.where(ei == i2, w2, 0.0))


def _router_call(h, w_router):
    t, d = h.shape
    ne = w_router.shape[1]
    tm = TOKEN_TILE
    return pl.pallas_call(
        _router_kernel, grid=(t // tm,),
        in_specs=[pl.BlockSpec((tm, d), lambda i: (i, 0)), pl.BlockSpec((ne, d), lambda i: (0, 0))],
        out_specs=[pl.BlockSpec((ne, tm), lambda i: (0, i)), pl.BlockSpec((ne, tm), lambda i: (0, i))],
        out_shape=[jax.ShapeDtypeStruct((ne, t), jnp.int32), jax.ShapeDtypeStruct((ne, t), F32)],
        compiler_params=_cparams(("parallel",)), name="router_top2",
    )(h, w_router.T)


def _gather_kernel(*refs, n_src, weighted, tm):
    nrows_ref = refs[0]
    idx_now = refs[1:1 + n_src]
    idx_next = refs[1 + n_src:1 + 2 * n_src]
    refs = refs[1 + 2 * n_src:]
    table = refs[0]
    w_ref = refs[1] if weighted else None
    o_ref, buf, sem = refs[-3:]
    i = pl.program_id(0)
    nrows = nrows_ref[0]

    def issue(idx_refs, slot):
        for k in range(n_src):
            def body(r8, carry, k=k):
                for u in range(GATHER_UNROLL):
                    r = r8 * GATHER_UNROLL + u
                    pltpu.make_async_copy(table.at[pl.ds(idx_refs[k][0, 0, r], 1), :],
                                          buf.at[slot, k, pl.ds(r, 1), :], sem.at[slot, k]).start(priority=u % 2)
                return carry
            lax.fori_loop(0, tm // GATHER_UNROLL, body, 0)

    slot = lax.rem(i, 2)
    valid = i * tm < nrows

    @pl.when((i == 0) & valid)
    def _():
        issue(idx_now, 0)

    @pl.when((i + 1) * tm < nrows)
    def _():
        issue(idx_next, 1 - slot)

    @pl.when(valid)
    def _():
        acc = None
        for k in range(n_src):
            pltpu.make_async_copy(table.at[pl.ds(0, tm), :], buf.at[slot, k], sem.at[slot, k]).wait()
            term = buf[slot, k] * w_ref[:, k:k + 1] if weighted else buf[slot, k]
            acc = term if acc is None else acc + term
        o_ref[...] = acc.astype(o_ref.dtype)

    @pl.when(jnp.logical_not(valid))
    def _():
        o_ref[...] = jnp.zeros(o_ref.shape, o_ref.dtype)


def _gather_call(table, idx, weights, out_dtype, n_rows, tm):
    n_src, t_out = idx.shape
    d = table.shape[1]
    nt = t_out // tm
    idx3 = idx.reshape(n_src, nt, 1, tm)
    smem = functools.partial(pl.BlockSpec, memory_space=pltpu.SMEM)
    in_specs = ([smem((1, 1, tm), lambda i, nr: (i, 0, 0)) for _ in range(n_src)]
                + [smem((1, 1, tm), lambda i, nr: (jnp.minimum(i + 1, nt - 1), 0, 0)) for _ in range(n_src)]
                + [pl.BlockSpec(memory_space=pl.ANY)])
    args = [idx3[k] for k in range(n_src)] * 2 + [table]
    if weights is not None:
        in_specs.append(pl.BlockSpec((tm, n_src), lambda i, nr: (i, 0)))
        args.append(weights)
    grid_spec = pltpu.PrefetchScalarGridSpec(
        num_scalar_prefetch=1, grid=(nt,), in_specs=in_specs,
        out_specs=pl.BlockSpec((tm, d), lambda i, nr: (i, 0)),
        scratch_shapes=[pltpu.VMEM((2, n_src, tm, d), table.dtype), pltpu.SemaphoreType.DMA((2, n_src))],
    )
    return pl.pallas_call(
        functools.partial(_gather_kernel, n_src=n_src, weighted=weights is not None, tm=tm),
        grid_spec=grid_spec, out_shape=jax.ShapeDtypeStruct((t_out, d), out_dtype),
        compiler_params=_cparams(("arbitrary",)), name="row_gather",
    )(jnp.asarray(n_rows, jnp.int32).reshape(1), *args)


def _moe(h, w_router, w_gate, w_up, w_down):
    t, d = h.shape
    ne = w_router.shape[1]
    tm = MOE_TILE_M
    sel, cw = _router_call(h, w_router)
    hot = (sel > 0).astype(jnp.int32)
    rank = jnp.cumsum(hot, axis=1) - hot
    counts = jnp.sum(hot, axis=1)
    tiles_e = (counts + tm - 1) // tm
    tile_end = jnp.cumsum(tiles_e)
    tile_start = tile_end - tiles_e
    n_valid = tile_end[-1]
    n_tiles = (2 * t + tm - 1) // tm + ne
    tp = n_tiles * tm
    dest = tile_start[:, None] * tm + rank
    tid = jnp.arange(n_tiles, dtype=jnp.int32)
    te = jnp.sum((tid[:, None] >= tile_end[None, :]).astype(jnp.int32), axis=1)
    last = jnp.maximum(n_valid - 1, 0)
    te = jnp.where(tid < n_valid, te, te[last]).astype(jnp.int32)
    xi = jnp.minimum(tid, last).astype(jnp.int32)
    p1 = jnp.sum(jnp.where(sel == 1, dest, 0), axis=0)
    p2 = jnp.sum(jnp.where(sel == 2, dest, 0), axis=0)
    wts = jnp.stack([jnp.sum(jnp.where(sel == 1, cw, 0.0), axis=0),
                     jnp.sum(jnp.where(sel == 2, cw, 0.0), axis=0)], axis=1)

    pidx = jnp.stack([p1, p2]).astype(jnp.int32)
    tok = jnp.arange(t, dtype=jnp.int32)
    src = jnp.zeros((tp,), jnp.int32).at[pidx.reshape(-1)].set(jnp.concatenate([tok, tok]), unique_indices=True)
    xs = _gather_call(h, src[None, :], None, BF16, n_valid * tm, MOE_GATHER_TILE)
    ys = _ffn_call(xs, w_gate, w_up, w_down, te, xi, n_valid.reshape(1).astype(jnp.int32), tm)
    return _gather_call(ys, pidx, wts, F32, t, GATHER_TILE)


_SRC_Z_SSD = 0
_SRC_X_SSD = SSD_WIDTH
_SRC_BC = 2 * SSD_WIDTH
_SRC_DT = SSD_WIDTH + SSD_CONV_DIM
_SRC_QKV = _SRC_DT + SSD_HEADS
_SRC_Z_GDN = _SRC_QKV + GDN_CONV_DIM
_SRC_A = _SRC_Z_GDN + GDN_WIDTH
_SRC_B = _SRC_A + GDN_V_HEADS
_SEGMENTS = ((COL_QKV, _SRC_QKV, GDN_CONV_DIM), (COL_Z_SSD, _SRC_Z_SSD, SSD_WIDTH),
             (COL_Z_GDN, _SRC_Z_GDN, GDN_WIDTH), (COL_X_SSD, _SRC_X_SSD, SSD_WIDTH),
             (COL_BC, _SRC_BC, SSD_CONV_DIM - SSD_WIDTH), (COL_SMALL + SMALL_DT, _SRC_DT, SSD_HEADS),
             (COL_SMALL + SMALL_A, _SRC_A, 2 * GDN_V_HEADS))
_USED_COLS = COL_SMALL + SMALL_B + GDN_V_HEADS


def _reorder_kernel(w_ref, o_ref):
    for dst, src, rows in _SEGMENTS:
        o_ref[0, dst:dst + rows, :] = w_ref[0, src:src + rows, :].astype(BF16)
    o_ref[0, _USED_COLS:PROJ_COLS, :] = jnp.zeros((PROJ_COLS - _USED_COLS, o_ref.shape[-1]), BF16)


def _reorder_in_proj(w_in):
    depth, d, n_in = w_in.shape
    tk = 512
    return pl.pallas_call(
        _reorder_kernel, grid=(depth, d // tk),
        in_specs=[pl.BlockSpec((1, n_in, tk), lambda l, k: (l, 0, k))],
        out_specs=pl.BlockSpec((1, PROJ_COLS, tk), lambda l, k: (l, 0, k)),
        out_shape=jax.ShapeDtypeStruct((depth, PROJ_COLS, d), BF16),
        compiler_params=_cparams(("parallel", "parallel")), name="reorder_w_in",
    )(jnp.swapaxes(w_in, 1, 2))


def kernel(x_prompt, x_sample, c_prompt, c_sample, state_ssd_conv, state_ssm, state_gdn_conv, state_gdn, w_ada, b_ada, w_norm_mix, w_norm_ffn, w_in, w_conv_ssd, b_conv_ssd, ssd_dt_bias, ssd_a_log, ssd_d, ssd_norm, w_conv_gdn, gdn_dt_bias, gdn_a_log, gdn_norm, w_out, w_ffn_gate, w_ffn_up, w_ffn_down, w_router, w_exp_gate, w_exp_up, w_exp_down, w_norm_final):
    bp, lp, d = x_prompt.shape
    bs, ls, _ = x_sample.shape
    depth = w_in.shape[0]
    tpr = bp * lp
    tsa = bs * ls
    t = tpr + tsa

    x = (x_prompt.reshape(tpr, d), x_sample.reshape(tsa, d))
    mod = _ada_call(jnp.concatenate([c_sample, c_prompt], axis=0), w_ada, b_ada)
    w_in_t = _reorder_in_proj(w_in)
    w_out_b = w_out.astype(BF16)

    qp = PROMPT_CHUNK if lp % PROMPT_CHUNK == 0 else lp
    nseq_s = 16 // ls if ls < 16 else 1
    sw = GDN_V_HEADS * GDN_HEAD
    zeros_p = (jnp.zeros((1, bp, CONV_WIDTH - 1, SSD_CONV_DIM), F32), jnp.zeros((1, bp, SSD_WIDTH, SSD_STATE), F32),
               jnp.zeros((1, bp, CONV_WIDTH - 1, GDN_CONV_DIM), F32), jnp.zeros((1, bp, sw, GDN_HEAD), F32))
    ssm_in = state_ssm.reshape(depth, bs, SSD_WIDTH, SSD_STATE)
    gdn_in = state_gdn.reshape(depth, bs, sw, GDN_HEAD)

    new_p = [[], [], [], []]
    new_s_conv = [[], []]
    ssm_out = gdn_out = None
    common = dict(t_prompt=tpr, l_prompt=lp)
    mods = [(mod[l, bs:].reshape(bp, 1, N_MOD * d), jnp.repeat(mod[l, :bs], ls, axis=0)) for l in range(depth)]
    h = _ew_call(x, w_norm_mix[0], mod=(1, 0) + mods[0], **common)
    for l in range(depth):
        mod_p, mod_s = mods[l]
        lw = {"w_conv_ssd": w_conv_ssd[l], "b_conv_ssd": b_conv_ssd[l], "ssd_dt_bias": ssd_dt_bias[l],
              "ssd_a_log": ssd_a_log[l], "ssd_d": ssd_d[l], "ssd_norm": ssd_norm[l],
              "w_conv_gdn": w_conv_gdn[l], "gdn_dt_bias": gdn_dt_bias[l], "gdn_a_log": gdn_a_log[l],
              "gdn_norm": gdn_norm[l]}
        proj = _in_proj_call(h, w_in_t, l)

        prompt = dict(row0=0, bsz=bp, seq=lp, q=qp, nseq=1)
        sample = dict(row0=tpr, bsz=bs, seq=ls, q=ls, nseq=nseq_s)
        yp_ssd, pc, pst = _ssd_call(proj, lw, zeros_p[0], zeros_p[1], 0, **prompt)
        ys_ssd, sc_, ssm_out = _ssd_call(proj, lw, state_ssd_conv, ssm_in, l, stacked=(depth, ssm_out), **sample)
        yp_gdn, pgc, pgst = _gdn_call(proj, lw, zeros_p[2], zeros_p[3], 0, **prompt)
        ys_gdn, sgc, gdn_out = _gdn_call(proj, lw, state_gdn_conv, gdn_in, l, stacked=(depth, gdn_out), **sample)
        for lst, v in zip(new_p, (pc, pst, pgc, pgst)):
            lst.append(v)
        new_s_conv[0].append(sc_)
        new_s_conv[1].append(sgc)

        mix = _out_proj_call(yp_ssd, yp_gdn, ys_ssd, ys_gdn, w_out_b[l])
        moe_layer = l % 2 == 1
        x, h2 = _ew_call(x, w_norm_ffn[l], res=(mix, 2, mod_p, mod_s), mod=(4, 3, mod_p, mod_s), out_x=True,
                         h_dtype=F32 if moe_layer else BF16, **common)
        i = l // 2
        if moe_layer:
            f = _moe(h2, w_router[i], w_exp_gate[i], w_exp_up[i], w_exp_down[i])
        else:
            nt = t // FFN_TILE_M
            tid = jnp.arange(nt, dtype=jnp.int32)
            f = _ffn_call(h2, w_ffn_gate[i][None], w_ffn_up[i][None], w_ffn_down[i][None],
                          jnp.zeros((nt,), jnp.int32), tid, jnp.full((1,), nt, jnp.int32), FFN_TILE_M)
        if l + 1 < depth:
            x, h = _ew_call(x, w_norm_mix[l + 1], res=(f, 5, mod_p, mod_s), mod=(1, 0) + mods[l + 1],
                            out_x=True, **common)
        else:
            y_p, y_s = _ew_call(x, w_norm_final, res=(f, 5, mod_p, mod_s), h_dtype=F32, split_out=True, **common)

    ssm_shape = (SSD_HEADS, SSD_HEAD_DIM, SSD_STATE)
    gdn_shape = (GDN_V_HEADS, GDN_HEAD, GDN_HEAD)
    prompt_states = (jnp.stack(new_p[0]), jnp.stack(new_p[1]).reshape((depth, bp) + ssm_shape),
                     jnp.stack(new_p[2]), jnp.stack(new_p[3]).reshape((depth, bp) + gdn_shape))
    sample_states = (jnp.stack(new_s_conv[0]), ssm_out.reshape((depth, bs) + ssm_shape),
                     jnp.stack(new_s_conv[1]), gdn_out.reshape((depth, bs) + gdn_shape))
    return (y_p.reshape(bp, lp, d), y_s.reshape(bs, ls, d)) + prompt_states + sample_states
```

```python
import functools

import jax
import jax.numpy as jnp
from jax import lax
from jax.experimental import pallas as pl
from jax.experimental.pallas import tpu as pltpu

F32 = jnp.float32
BF16 = jnp.bfloat16

D_MODEL = 2048
CONV_WIDTH = 4
SSD_WIDTH = 1024
SSD_HEAD_DIM = 64
SSD_HEADS = 16
SSD_GROUPS = 2
SSD_STATE = 128
SSD_CONV_DIM = SSD_WIDTH + 2 * SSD_GROUPS * SSD_STATE
GDN_WIDTH = 1024
GDN_HEAD = 128
GDN_V_HEADS = 8
GDN_K_HEADS = 4
GDN_QK_DIM = GDN_K_HEADS * GDN_HEAD
GDN_CONV_DIM = 2 * GDN_QK_DIM + GDN_WIDTH
N_EXPERTS = 8
N_MOD = 6
EPS = 1e-6

PROJ_COLS = 5760
COL_QKV, COL_Z_SSD, COL_Z_GDN, COL_X_SSD, COL_BC, COL_SMALL = 0, 2048, 3072, 4096, 5120, 5632
SMALL_DT, SMALL_A, SMALL_B = 0, SSD_HEADS, SSD_HEADS + GDN_V_HEADS

V7X_VMEM_LIMIT = 56 * 1024 * 1024
TOKEN_TILE = 256
MM_TILE_M = 1024
IN_PROJ_TILE_N = 1920
FFN_TILE_M = 768
MOE_TILE_M = 768
FFN_TILE_F = 512
GATHER_TILE = 256
GATHER_UNROLL = 8
PROMPT_CHUNK = 128
NEG_BIG = -1e30


def _cparams(sem):
    return pltpu.CompilerParams(dimension_semantics=sem, vmem_limit_bytes=V7X_VMEM_LIMIT)


_NN = (((1,), (0,)), ((), ()))
_NT = (((1,), (1,)), ((), ()))
_TN = (((0,), (0,)), ((), ()))


def _dg(a, b, dims):
    return lax.dot_general(a, b, dims, preferred_element_type=F32)


def _dot(a, b, dims=_NN):
    return _dg(a.astype(BF16), b.astype(BF16), dims)


def _split(x, n):
    parts = []
    r = x
    for _ in range(n - 1):
        p = r.astype(BF16)
        parts.append(p)
        r = r - p.astype(F32)
    parts.append(r.astype(BF16))
    return parts


def _dot_exact_rhs(a, b, dims=_NN):
    bb = b.astype(BF16)
    a1, a2, a3 = _split(a, 3)
    return _dg(a3, bb, dims) + _dg(a2, bb, dims) + _dg(a1, bb, dims)


def _dot_exact_lhs(a, b, dims=_NN):
    ab = a.astype(BF16)
    b1, b2, b3 = _split(b, 3)
    return _dg(ab, b3, dims) + _dg(ab, b2, dims) + _dg(ab, b1, dims)


def _dot6(a, b, dims=_NN):
    a1, a2, a3 = _split(a, 3)
    b1, b2, b3 = _split(b, 3)
    small = _dg(a1, b3, dims) + _dg(a2, b2, dims) + _dg(a3, b1, dims)
    mid = _dg(a1, b2, dims) + _dg(a2, b1, dims)
    return small + mid + _dg(a1, b1, dims)


def _sigmoid(x):
    return 1.0 / (1.0 + jnp.exp(-x))


def _silu(x):
    return x * _sigmoid(x)


def _softplus(x):
    return jnp.maximum(x, 0.0) + jnp.log1p(jnp.exp(-jnp.abs(x)))


def _iota(shape, dim):
    return lax.broadcasted_iota(jnp.int32, shape, dim)


def _transpose_cols(x, n):
    eye = (_iota((n, n), 0) == _iota((n, n), 1)).astype(F32)
    return _dot_exact_lhs(eye, x, _NT)


def _conv_silu(u, ext_ref, s, w_ref, bias, q):
    ext_ref[s, 8:8 + q, :] = u
    acc = u * w_ref[CONV_WIDTH - 1:CONV_WIDTH, :]
    if bias is not None:
        acc = acc + bias
    for j in range(1, CONV_WIDTH):
        acc = acc + ext_ref[s, 8 - j:8 - j + q, :] * w_ref[CONV_WIDTH - 1 - j:CONV_WIDTH - j, :]
    return _silu(acc)


def _conv_advance(ext_ref, s, q):
    ext_ref[s, 0:8, :] = ext_ref[s, q:q + 8, :]


def _conv_init(ext_ref, s, state_rows):
    ext_ref[s, 0:8, :] = jnp.zeros((8, ext_ref.shape[-1]), F32)
    ext_ref[s, 8 - (CONV_WIDTH - 1):8, :] = state_rows


def _conv_tail(ext_ref, s, q):
    return ext_ref[s, q + 8 - (CONV_WIDTH - 1):q + 8, :]


def _ada_kernel(c_ref, w_ref, b_ref, o_ref):
    a = _silu(c_ref[...]).astype(BF16)
    o_ref[0] = jnp.dot(a, w_ref[0].astype(BF16), preferred_element_type=F32) + b_ref[0]


def _ada_call(c_all, w_ada, b_ada):
    depth, d, n = w_ada.shape
    m = c_all.shape[0]
    tn = 1024
    return pl.pallas_call(
        _ada_kernel,
        grid=(depth, n // tn),
        in_specs=[
            pl.BlockSpec((m, d), lambda l, j: (0, 0)),
            pl.BlockSpec((1, d, tn), lambda l, j: (l, 0, j)),
            pl.BlockSpec((1, 1, tn), lambda l, j: (l, 0, j)),
        ],
        out_specs=pl.BlockSpec((1, m, tn), lambda l, j: (l, 0, j)),
        out_shape=jax.ShapeDtypeStruct((depth, m, n), F32),
        compiler_params=_cparams(("parallel", "parallel")),
        name="ada_mod",
    )(c_all, w_ada, b_ada.reshape(depth, 1, n))


def _issue_row_gathers(table, idx_refs, buf, sem, slot, tm):
    for k, idx_ref in enumerate(idx_refs):
        def body(r8, carry, k=k, idx_ref=idx_ref):
            for u in range(GATHER_UNROLL):
                r = r8 * GATHER_UNROLL + u
                pltpu.make_async_copy(table.at[pl.ds(idx_ref[0, 0, r], 1), :],
                                      buf.at[slot, k, pl.ds(r, 1), :], sem.at[slot, k]).start()
            return carry
        lax.fori_loop(0, tm // GATHER_UNROLL, body, 0)


def _wait_row_gathers(table, buf, sem, slot, k, tm):
    pltpu.make_async_copy(table.at[pl.ds(0, tm), :], buf.at[slot, k], sem.at[slot, k]).wait()


def _route_top2(h, wr_ref, sel_ref, cw_ref):
    logits = _dot6(wr_ref[...], h, _NT)
    ne = logits.shape[0]
    ei = _iota(logits.shape, 0)
    m1 = jnp.max(logits, axis=0, keepdims=True)
    i1 = jnp.min(jnp.where(logits == m1, ei, ne), axis=0, keepdims=True)
    rest = jnp.where(ei == i1, -jnp.inf, logits)
    m2 = jnp.max(rest, axis=0, keepdims=True)
    i2 = jnp.min(jnp.where(rest == m2, ei, ne), axis=0, keepdims=True)
    e = jnp.exp(m2 - m1)
    w1 = 1.0 / (1.0 + e)
    w2 = e / (1.0 + e)
    sel_ref[...] = jnp.where(ei == i1, 1, jnp.where(ei == i2, 2, 0)).astype(jnp.int32)
    cw_ref[...] = jnp.where(ei == i1, w1, jnp.where(ei == i2, w2, 0.0))


def _ew_kernel(*refs, split_in, has_res, n_gather, modded, routed, out_x, split_out, n_prompt_tiles):
    refs = list(refs)
    xp_ref = refs.pop(0)
    xs_ref = refs.pop(0) if split_in else xp_ref
    if n_gather:
        idx_now = [refs.pop(0) for _ in range(n_gather)]
        idx_next = [refs.pop(0) for _ in range(n_gather)]
        table, wts_ref = refs.pop(0), refs.pop(0)
    elif has_res:
        y_ref = refs.pop(0)
    if has_res:
        gp_ref, gs_ref = refs.pop(0), refs.pop(0)
    w_ref = refs.pop(0)
    if modded:
        scp_ref, shp_ref, scs_ref, shs_ref = refs.pop(0), refs.pop(0), refs.pop(0), refs.pop(0)
    if routed:
        wr_ref = refs.pop(0)
    if out_x:
        xo_ref = refs.pop(0)
    hp_ref = refs.pop(0)
    hs_ref = refs.pop(0) if split_out else hp_ref
    if routed:
        sel_ref, cw_ref = refs.pop(0), refs.pop(0)
    i = pl.program_id(0)

    y = None
    if n_gather:
        buf, sem = refs.pop(0), refs.pop(0)
        tm = buf.shape[2]
        slot = lax.rem(i, 2)

        @pl.when(i == 0)
        def _():
            _issue_row_gathers(table, idx_now, buf, sem, 0, tm)

        @pl.when(i + 1 < pl.num_programs(0))
        def _():
            _issue_row_gathers(table, idx_next, buf, sem, 1 - slot, tm)

        for k in range(n_gather):
            _wait_row_gathers(table, buf, sem, slot, k, tm)
            term = buf[slot, k] * wts_ref[:, k:k + 1]
            y = term if y is None else y + term
    elif has_res:
        y = y_ref[...]

    def run(prompt):
        x = (xp_ref if prompt else xs_ref)[...]
        if has_res:
            g = gp_ref[0] if prompt else gs_ref[...]
            x = x + g * y
            if out_x:
                xo_ref[...] = x
        h = x * lax.rsqrt(jnp.mean(x * x, axis=-1, keepdims=True) + EPS) * w_ref[...]
        if modded:
            sc = scp_ref[0] if prompt else scs_ref[...]
            sh = shp_ref[0] if prompt else shs_ref[...]
            h = h * (1.0 + sc) + sh
        h_ref = hp_ref if prompt else hs_ref
        h_ref[...] = h.astype(h_ref.dtype)
        if routed:
            _route_top2(h, wr_ref, sel_ref, cw_ref)

    @pl.when(i < n_prompt_tiles)
    def _():
        run(True)

    @pl.when(i >= n_prompt_tiles)
    def _():
        run(False)


def _ew_call(x, w, *, t_prompt, l_prompt, res=None, mod=None, out_x=False, h_dtype=BF16, split_out=False,
             router_w=None):
    split_in = isinstance(x, tuple)
    d = w.shape[0]
    t = sum(a.shape[0] for a in x) if split_in else x.shape[0]
    tm = TOKEN_TILE
    npt = t_prompt // tm
    per_seq = l_prompt // tm

    def pspec(chunk, mp):
        bp = mp.shape[0]
        return pl.BlockSpec((1, 1, d), lambda i: (jnp.minimum(i // per_seq, bp - 1), 0, chunk))

    def sspec(chunk):
        return pl.BlockSpec((tm, d), lambda i: (jnp.maximum(i - npt, 0), chunk))

    row = pl.BlockSpec((tm, d), lambda i: (i, 0))
    prow = pl.BlockSpec((tm, d), lambda i: (jnp.minimum(i, npt - 1), 0))
    srow = sspec(0)
    args, specs = (list(x), [prow, srow]) if split_in else ([x], [row])
    n_gather = 0
    scratch = []
    if res is not None:
        y, gch, gmp, gms = res
        if isinstance(y, tuple):
            table, idx, wts = y
            n_gather = idx.shape[0]
            nt = t // tm
            idx3 = idx.reshape(n_gather, nt, 1, tm)
            smem = functools.partial(pl.BlockSpec, memory_space=pltpu.SMEM)
            args += [idx3[k] for k in range(n_gather)] * 2 + [table, wts]
            specs += ([smem((1, 1, tm), lambda i: (i, 0, 0)) for _ in range(n_gather)]
                      + [smem((1, 1, tm), lambda i: (jnp.minimum(i + 1, nt - 1), 0, 0)) for _ in range(n_gather)]
                      + [pl.BlockSpec(memory_space=pl.ANY), pl.BlockSpec((tm, n_gather), lambda i: (i, 0))])
            scratch = [pltpu.VMEM((2, n_gather, tm, d), table.dtype), pltpu.SemaphoreType.DMA((2, n_gather))]
        else:
            args.append(y)
            specs.append(row)
        args += [gmp, gms]
        specs += [pspec(gch, gmp), sspec(gch)]
    args.append(w.reshape(1, d))
    specs.append(pl.BlockSpec((1, d), lambda i: (0, 0)))
    if mod is not None:
        scc, shc, mmp, mms = mod
        args += [mmp, mmp, mms, mms]
        specs += [pspec(scc, mmp), pspec(shc, mmp), sspec(scc), sspec(shc)]
    if router_w is not None:
        ne = router_w.shape[1]
        args.append(router_w.T)
        specs.append(pl.BlockSpec((ne, d), lambda i: (0, 0)))
    out_shape, out_specs = [], []
    if out_x:
        out_shape.append(jax.ShapeDtypeStruct((t, d), F32))
        out_specs.append(row)
    if split_out:
        out_shape += [jax.ShapeDtypeStruct((t_prompt, d), h_dtype), jax.ShapeDtypeStruct((t - t_prompt, d), h_dtype)]
        out_specs += [prow, srow]
    else:
        out_shape.append(jax.ShapeDtypeStruct((t, d), h_dtype))
        out_specs.append(row)
    if router_w is not None:
        out_shape += [jax.ShapeDtypeStruct((ne, t), jnp.int32), jax.ShapeDtypeStruct((ne, t), F32)]
        out_specs += [pl.BlockSpec((ne, tm), lambda i: (0, i)), pl.BlockSpec((ne, tm), lambda i: (0, i))]
    kern = functools.partial(_ew_kernel, split_in=split_in, has_res=res is not None, n_gather=n_gather,
                             modded=mod is not None, routed=router_w is not None, out_x=out_x,
                             split_out=split_out, n_prompt_tiles=npt)
    outs = pl.pallas_call(
        kern, grid=(t // tm,), in_specs=specs, out_specs=out_specs, out_shape=out_shape,
        scratch_shapes=scratch,
        compiler_params=_cparams(("arbitrary",)), name="token_norm",
    )(*args)
    return outs if len(outs) > 1 else outs[0]


def _in_proj_kernel(a_ref, wt_ref, o_ref):
    o_ref[...] = _dg(a_ref[...], wt_ref[...], _NT)


def _in_proj_call(a, w_t, layer):
    m, k = a.shape
    n = w_t.shape[1]
    tn = IN_PROJ_TILE_N
    tm = MM_TILE_M
    while m % tm:
        tm //= 2
    return pl.pallas_call(
        _in_proj_kernel,
        grid=(m // tm, n // tn),
        in_specs=[pl.BlockSpec((tm, k), lambda i, j: (i, 0)),
                  pl.BlockSpec((None, tn, k), lambda i, j: (layer, j, 0))],
        out_specs=pl.BlockSpec((tm, tn), lambda i, j: (i, j)),
        out_shape=jax.ShapeDtypeStruct((m, n), F32),
        compiler_params=_cparams(("parallel", "parallel")),
        name="in_proj",
    )(a, w_t)


def _out_proj_kernel(ap1_ref, ap2_ref, as1_ref, as2_ref, w1_ref, w2_ref, o_ref, *, n_prompt_tiles):
    i = pl.program_id(0)

    def run(a1_ref, a2_ref):
        o_ref[...] = (jnp.dot(a1_ref[...], w1_ref[...], preferred_element_type=F32)
                      + jnp.dot(a2_ref[...], w2_ref[...], preferred_element_type=F32))

    @pl.when(i < n_prompt_tiles)
    def _():
        run(ap1_ref, ap2_ref)

    @pl.when(i >= n_prompt_tiles)
    def _():
        run(as1_ref, as2_ref)


def _out_proj_call(yp_ssd, yp_gdn, ys_ssd, ys_gdn, w_out):
    tp, k1 = yp_ssd.shape
    ts = ys_ssd.shape[0]
    k2 = yp_gdn.shape[1]
    n = w_out.shape[1]
    tm = MM_TILE_M
    while tp % tm or ts % tm:
        tm //= 2
    tn = 1024
    npt = tp // tm
    pmap = lambda i, j: (jnp.minimum(i, npt - 1), 0)
    smap = lambda i, j: (jnp.maximum(i - npt, 0), 0)
    return pl.pallas_call(
        functools.partial(_out_proj_kernel, n_prompt_tiles=npt),
        grid=((tp + ts) // tm, n // tn),
        in_specs=[pl.BlockSpec((tm, k1), pmap), pl.BlockSpec((tm, k2), pmap),
                  pl.BlockSpec((tm, k1), smap), pl.BlockSpec((tm, k2), smap),
                  pl.BlockSpec((k1, tn), lambda i, j: (0, j)),
                  pl.BlockSpec((k2, tn), lambda i, j: (k1 // k2, j))],
        out_specs=pl.BlockSpec((tm, tn), lambda i, j: (i, j)),
        out_shape=jax.ShapeDtypeStruct((tp + ts, n), F32),
        compiler_params=_cparams(("parallel", "parallel")),
        name="out_proj",
    )(yp_ssd, yp_gdn, ys_ssd, ys_gdn, w_out, w_out)


def _store_state(sto_ref, s, value, stack):
    if stack == "first":
        sto_ref[0, s] = value
        for l in range(1, sto_ref.shape[0]):
            sto_ref[l, s] = jnp.zeros(value.shape, value.dtype)
    else:
        sto_ref[s] = value


def _ssd_kernel(*refs, q, nseq, nc, stack):
    (z_ref, xs_ref, bc_ref, sm_ref, wx_ref, wbc_ref, bx_ref, bbc_ref, dtb_ref, alog_ref,
     dexp_ref, nw_ref, cst_ref, st_ref) = refs[:14]
    y_ref, cso_ref, sto_ref, extx, extbc, s_scr = refs[14 + (stack == "next"):]
    c = pl.program_id(1)
    nh = SSD_HEADS
    hg = SSD_HEADS // SSD_GROUPS
    wg = SSD_WIDTH // SSD_GROUPS
    p = SSD_HEAD_DIM
    n = SSD_STATE
    gn = SSD_GROUPS * n

    li = _iota((q, q), 0)
    si = _iota((q, q), 1)
    tril = si <= li
    tri = tril.astype(F32)
    expand = (_iota((nh, SSD_WIDTH), 1) // p == _iota((nh, SSD_WIDTH), 0)).astype(F32)
    expand_t = (_iota((SSD_WIDTH, nh), 0) // p == _iota((SSD_WIDTH, nh), 1)).astype(F32)
    lane = _iota((q, 128), 1)

    for s in range(nseq):
        rows = slice(s * q, (s + 1) * q)

        @pl.when(c == 0)
        def _():
            _conv_init(extx, s, cst_ref[s, :, 0:SSD_WIDTH])
            _conv_init(extbc, s, cst_ref[s, :, SSD_WIDTH:SSD_CONV_DIM])
            s_scr[s] = st_ref[s]

        xs = _conv_silu(xs_ref[rows, :], extx, s, wx_ref, bx_ref[...], q)
        bcv = _conv_silu(bc_ref[rows, :], extbc, s, wbc_ref, bbc_ref[...], q)

        @pl.when(c == nc - 1)
        def _():
            cso_ref[s, :, 0:SSD_WIDTH] = _conv_tail(extx, s, q)
            cso_ref[s, :, SSD_WIDTH:SSD_CONV_DIM] = _conv_tail(extbc, s, q)

        _conv_advance(extx, s, q)
        _conv_advance(extbc, s, q)

        dt = _softplus(sm_ref[rows, :][:, SMALL_DT:SMALL_DT + nh] + dtb_ref[...])
        a = dt * (-jnp.exp(alog_ref[...]))
        a_cum = _dot_exact_lhs(tri, a)
        a_cum_t = _transpose_cols(a_cum, nh)
        dt_e = _dot_exact_rhs(dt, expand)
        acum_e = _dot_exact_rhs(a_cum, expand)
        xdt = xs * dt_e
        alast_e = acum_e[q - 1:q, :]
        xd = xdt * jnp.exp(alast_e - acum_e)
        alast_b = jnp.broadcast_to(a_cum_t[:, q - 1:q], (nh, 128))
        dec_rows = jnp.exp(_dot_exact_lhs(expand_t, alast_b))

        pieces = []
        for g in range(SSD_GROUPS):
            bm = bcv[:, g * n:(g + 1) * n]
            cm = bcv[:, gn + g * n:gn + (g + 1) * n]
            cb = _dot(cm, bm, _NT)
            s_old = s_scr[s, g * wg:(g + 1) * wg, :]
            pieces.append(_dot(cm, s_old, _NT))
            s_scr[s, g * wg:(g + 1) * wg, :] = (dec_rows[g * wg:(g + 1) * wg, :] * s_old
                                                + _dot(xd[:, g * wg:(g + 1) * wg], bm, _TN))
            for k in range(hg // 2):
                ms = []
                for hh in (g * hg + 2 * k, g * hg + 2 * k + 1):
                    seg = a_cum[:, hh:hh + 1] - a_cum_t[hh:hh + 1, :]
                    ms.append(cb * jnp.exp(jnp.where(tril, seg, NEG_BIG)))
                c0 = g * wg + 2 * p * k
                xp = xdt[:, c0:c0 + 2 * p]
                top = jnp.where(lane < p, xp, 0.0)
                bot = jnp.where(lane >= p, xp, 0.0)
                if q % 128 == 0:
                    piece = _dot(jnp.concatenate(ms, axis=1), jnp.concatenate([top, bot], axis=0))
                else:
                    piece = _dot(ms[0], top) + _dot(ms[1], bot)
                pieces.append(piece)
        y_off = jnp.concatenate([pieces[0], pieces[1 + hg // 2]], axis=1) * jnp.exp(acum_e)
        y_diag = jnp.concatenate(pieces[1:1 + hg // 2] + pieces[2 + hg // 2:], axis=1)
        y = y_off + y_diag + dexp_ref[...] * xs
        y = y * _silu(z_ref[rows, :])
        outs = []
        for g in range(SSD_GROUPS):
            yg = y[:, g * wg:(g + 1) * wg]
            outs.append(yg * lax.rsqrt(jnp.mean(yg * yg, axis=-1, keepdims=True) + EPS))
        y_ref[rows, :] = (jnp.concatenate(outs, axis=1) * nw_ref[...]).astype(y_ref.dtype)

        @pl.when(c == nc - 1)
        def _():
            _store_state(sto_ref, s, s_scr[s], stack)


def _state_out(stacked, layer, bsz, nseq, rows, n, in_specs, args):
    if stacked is None:
        return (jax.ShapeDtypeStruct((bsz, rows, n), F32),
                pl.BlockSpec((nseq, rows, n), lambda b, c: (b, 0, 0)), None, {})
    depth, prev = stacked
    shape = jax.ShapeDtypeStruct((depth, bsz, rows, n), F32)
    if prev is None:
        return shape, pl.BlockSpec((depth, nseq, rows, n), lambda b, c: (0, b, 0, 0)), "first", {}
    in_specs.append(pl.BlockSpec(memory_space=pl.ANY))
    args.append(prev)
    return (shape, pl.BlockSpec((None, nseq, rows, n), lambda b, c: (layer, b, 0, 0)), "next",
            {len(args) - 1: 2})


def _ssd_call(proj, lw, conv_state, ssm_state, layer, *, row0, bsz, seq, q, nseq, stacked=None):
    nc = seq // q
    r = nseq * q
    assert nseq == 1 or nc == 1
    rb0 = row0 // r
    n = SSD_STATE
    bcw = SSD_CONV_DIM - SSD_WIDTH
    grid = (bsz // nseq, nc)
    rowi = lambda b, c: rb0 + b * nc + c
    full = lambda shape: pl.BlockSpec(shape, lambda b, c: (0,) * len(shape))

    in_specs = [
        pl.BlockSpec((r, SSD_WIDTH), lambda b, c: (rowi(b, c), COL_Z_SSD // SSD_WIDTH)),
        pl.BlockSpec((r, SSD_WIDTH), lambda b, c: (rowi(b, c), COL_X_SSD // SSD_WIDTH)),
        pl.BlockSpec((r, bcw), lambda b, c: (rowi(b, c), COL_BC // bcw)),
        pl.BlockSpec((r, 128), lambda b, c: (rowi(b, c), COL_SMALL // 128)),
        pl.BlockSpec((CONV_WIDTH, SSD_WIDTH), lambda b, c: (0, 0)),
        pl.BlockSpec((CONV_WIDTH, bcw), lambda b, c: (0, SSD_WIDTH // bcw)),
        pl.BlockSpec((1, SSD_WIDTH), lambda b, c: (0, 0)),
        pl.BlockSpec((1, bcw), lambda b, c: (0, SSD_WIDTH // bcw)),
        full((1, SSD_HEADS)), full((1, SSD_HEADS)), full((1, SSD_WIDTH)), full((1, SSD_WIDTH)),
        pl.BlockSpec((None, nseq, CONV_WIDTH - 1, SSD_CONV_DIM), lambda b, c: (layer, b, 0, 0)),
        pl.BlockSpec((None, nseq, SSD_WIDTH, n), lambda b, c: (layer, b, 0, 0)),
    ]
    cw = lw["w_conv_ssd"]
    cbias = lw["b_conv_ssd"].reshape(1, SSD_CONV_DIM)
    args = [proj, proj, proj, proj, cw, cw, cbias, cbias,
            lw["ssd_dt_bias"].reshape(1, SSD_HEADS), lw["ssd_a_log"].reshape(1, SSD_HEADS),
            jnp.repeat(lw["ssd_d"], SSD_HEAD_DIM).reshape(1, SSD_WIDTH), lw["ssd_norm"].reshape(1, SSD_WIDTH),
            conv_state, ssm_state]
    st_shape, st_spec, stack, aliases = _state_out(stacked, layer, bsz, nseq, SSD_WIDTH, n, in_specs, args)
    out_shape = [
        jax.ShapeDtypeStruct((bsz * seq, SSD_WIDTH), BF16),
        jax.ShapeDtypeStruct((bsz, CONV_WIDTH - 1, SSD_CONV_DIM), F32),
        st_shape,
    ]
    out_specs = [
        pl.BlockSpec((r, SSD_WIDTH), lambda b, c: (b * nc + c, 0)),
        pl.BlockSpec((nseq, CONV_WIDTH - 1, SSD_CONV_DIM), lambda b, c: (b, 0, 0)),
        st_spec,
    ]
    scratch = [pltpu.VMEM((nseq, q + 8, SSD_WIDTH), F32), pltpu.VMEM((nseq, q + 8, bcw), F32),
               pltpu.VMEM((nseq, SSD_WIDTH, n), F32)]
    kern = functools.partial(_ssd_kernel, q=q, nseq=nseq, nc=nc, stack=stack)
    return pl.pallas_call(
        kern, grid=grid, in_specs=in_specs, out_specs=out_specs, out_shape=out_shape,
        scratch_shapes=scratch, input_output_aliases=aliases,
        compiler_params=_cparams(("parallel", "arbitrary")), name="ssd_mixer",
    )(*args)


def _neumann_inverse(xs, nil):
    n = xs[0].shape[0]
    eye = (_iota((n, n), 0) == _iota((n, n), 1)).astype(F32)
    invs = [eye - x for x in xs]
    pws = xs
    k = 2
    while k < nil:
        pws = [_dot(pw, pw) for pw in pws]
        invs = [inv + _dot(inv, pw) for inv, pw in zip(invs, pws)]
        k *= 2
    return invs


def _unit_lower_inverse(ms, q):
    blk = 16
    if q <= blk:
        return _neumann_inverse(ms, q)
    same = (_iota((q, q), 0) // blk) == (_iota((q, q), 1) // blk)
    dgs = [jnp.where(same, m, 0.0) for m in ms]
    dinvs = _neumann_inverse(dgs, blk)
    nns = [_dot(dinv, m - dg) for dinv, m, dg in zip(dinvs, ms, dgs)]
    ninvs = _neumann_inverse(nns, q // blk)
    return [_dot(ninv, dinv) for ninv, dinv in zip(ninvs, dinvs)]


def _gdn_kernel(*refs, q, nseq, nc, stack):
    qkv_ref, z_ref, sm_ref, w_ref, dtb_ref, alog_ref, nw_ref, cst_ref, st_ref = refs[:9]
    y_ref, cso_ref, sto_ref, ext, s_scr = refs[9 + (stack == "next"):]
    c = pl.program_id(1)
    nv = GDN_V_HEADS
    rep = GDN_V_HEADS // GDN_K_HEADS
    dk = GDN_HEAD
    dv = GDN_HEAD

    li = _iota((q, q), 0)
    si = _iota((q, q), 1)
    tril = si <= li
    strict = si < li
    tri = tril.astype(F32)

    heads = []
    for s in range(nseq):
        rows = slice(s * q, (s + 1) * q)

        @pl.when(c == 0)
        def _():
            _conv_init(ext, s, cst_ref[s])
            s_scr[s] = st_ref[s]

        xc = _conv_silu(qkv_ref[rows, :], ext, s, w_ref, None, q)

        @pl.when(c == nc - 1)
        def _():
            cso_ref[s] = _conv_tail(ext, s, q)

        _conv_advance(ext, s, q)

        sm = sm_ref[rows, :]
        a_raw = sm[:, SMALL_A:SMALL_A + nv]
        b_raw = sm[:, SMALL_B:SMALL_B + nv]
        gate = -jnp.exp(alog_ref[...]) * _softplus(a_raw + dtb_ref[...])
        g_cum = _dot_exact_lhs(tri, gate)
        g_cum_t = _transpose_cols(g_cum, nv)
        beta_all = _sigmoid(b_raw)

        for kh in range(GDN_K_HEADS):
            qc = xc[:, kh * dk:(kh + 1) * dk]
            kc = xc[:, GDN_QK_DIM + kh * dk:GDN_QK_DIM + (kh + 1) * dk]
            qn = qc * lax.rsqrt(jnp.sum(qc * qc, axis=-1, keepdims=True) + EPS) * (dk ** -0.5)
            kn = kc * lax.rsqrt(jnp.sum(kc * kc, axis=-1, keepdims=True) + EPS)
            kk = _dot(kn, kn, _NT)
            qk = _dot(qn, kn, _NT)
            for j in range(kh * rep, (kh + 1) * rep):
                col = g_cum[:, j:j + 1]
                seg = col - g_cum_t[j:j + 1, :]
                decay = jnp.exp(jnp.where(tril, seg, NEG_BIG))
                beta = beta_all[:, j:j + 1]
                e_col = jnp.exp(col)
                g_last = g_cum[q - 1:q, j:j + 1]
                vj = xc[:, 2 * GDN_QK_DIM + j * dv:2 * GDN_QK_DIM + (j + 1) * dv]
                heads.append(dict(
                    s=s, j=j, rows=rows,
                    m=jnp.where(strict, beta * kk * decay, 0.0),
                    rhs=jnp.concatenate([vj * beta, kn * (beta * e_col)], axis=1),
                    attn=qk * decay, q_dec=qn * e_col, k_dec=kn * jnp.exp(g_last - col),
                    s_dec=jnp.exp(g_last)))

    t_invs = _unit_lower_inverse([hd["m"] for hd in heads], q)
    sols = [_dot(t_inv, hd["rhs"]) for t_inv, hd in zip(t_invs, heads)]
    states = [s_scr[hd["s"], dk * hd["j"]:dk * (hd["j"] + 1), :] for hd in heads]
    boths = [_dot(jnp.concatenate([sol[:, dv:], hd["q_dec"]], axis=0), st)
             for sol, hd, st in zip(sols, heads, states)]
    v_news = [sol[:, :dv] - both[:q] for sol, both in zip(sols, boths)]
    outs = [both[q:] + _dot(hd["attn"], v_new) for both, hd, v_new in zip(boths, heads, v_news)]
    for hd, st, v_new in zip(heads, states, v_news):
        s_scr[hd["s"], dk * hd["j"]:dk * (hd["j"] + 1), :] = st * hd["s_dec"] + _dot(hd["k_dec"], v_new, _TN)
    for s in range(nseq):
        rows = slice(s * q, (s + 1) * q)
        ys = []
        for hd, o in zip(heads, outs):
            if hd["s"] == s:
                o = o * lax.rsqrt(jnp.mean(o * o, axis=-1, keepdims=True) + EPS) * nw_ref[...]
                ys.append(o * _silu(z_ref[rows, dv * hd["j"]:dv * (hd["j"] + 1)]))
        y_ref[rows, :] = jnp.concatenate(ys, axis=1).astype(y_ref.dtype)

        @pl.when(c == nc - 1)
        def _():
            _store_state(sto_ref, s, s_scr[s], stack)


def _gdn_call(proj, lw, conv_state, gdn_state, layer, *, row0, bsz, seq, q, nseq, stacked=None):
    nc = seq // q
    r = nseq * q
    assert nseq == 1 or nc == 1
    rb0 = row0 // r
    hd = GDN_HEAD
    sw = GDN_V_HEADS * hd
    grid = (bsz // nseq, nc)
    rowi = lambda b, c: rb0 + b * nc + c
    full = lambda shape: pl.BlockSpec(shape, lambda b, c: (0,) * len(shape))

    in_specs = [
        pl.BlockSpec((r, GDN_CONV_DIM), lambda b, c: (rowi(b, c), COL_QKV // GDN_CONV_DIM)),
        pl.BlockSpec((r, GDN_WIDTH), lambda b, c: (rowi(b, c), COL_Z_GDN // GDN_WIDTH)),
        pl.BlockSpec((r, 128), lambda b, c: (rowi(b, c), COL_SMALL // 128)),
        full((CONV_WIDTH, GDN_CONV_DIM)), full((1, GDN_V_HEADS)), full((1, GDN_V_HEADS)), full((1, hd)),
        pl.BlockSpec((None, nseq, CONV_WIDTH - 1, GDN_CONV_DIM), lambda b, c: (layer, b, 0, 0)),
        pl.BlockSpec((None, nseq, sw, hd), lambda b, c: (layer, b, 0, 0)),
    ]
    args = [proj, proj, proj, lw["w_conv_gdn"], lw["gdn_dt_bias"].reshape(1, GDN_V_HEADS),
            lw["gdn_a_log"].reshape(1, GDN_V_HEADS), lw["gdn_norm"].reshape(1, hd), conv_state, gdn_state]
    st_shape, st_spec, stack, aliases = _state_out(stacked, layer, bsz, nseq, sw, hd, in_specs, args)
    out_shape = [
        jax.ShapeDtypeStruct((bsz * seq, GDN_WIDTH), BF16),
        jax.ShapeDtypeStruct((bsz, CONV_WIDTH - 1, GDN_CONV_DIM), F32),
        st_shape,
    ]
    out_specs = [
        pl.BlockSpec((r, GDN_WIDTH), lambda b, c: (b * nc + c, 0)),
        pl.BlockSpec((nseq, CONV_WIDTH - 1, GDN_CONV_DIM), lambda b, c: (b, 0, 0)),
        st_spec,
    ]
    scratch = [pltpu.VMEM((nseq, q + 8, GDN_CONV_DIM), F32), pltpu.VMEM((nseq, sw, hd), F32)]
    kern = functools.partial(_gdn_kernel, q=q, nseq=nseq, nc=nc, stack=stack)
    return pl.pallas_call(
        kern, grid=grid, in_specs=in_specs, out_specs=out_specs, out_shape=out_shape,
        scratch_shapes=scratch, input_output_aliases=aliases,
        compiler_params=_cparams(("parallel", "arbitrary")), name="gdn_mixer",
    )(*args)


def _ffn_kernel(te_ref, xi_ref, nv_ref, x_ref, wg_ref, wu_ref, wd_ref, o_ref, *scratch):
    i = pl.program_id(0)
    j = pl.program_id(1)
    valid = i < nv_ref[0]

    @pl.when(j == 0)
    def _():
        o_ref[...] = jnp.zeros(o_ref.shape, o_ref.dtype)
        if scratch:
            scratch[0][...] = x_ref[...].astype(BF16)

    @pl.when(valid)
    def _():
        x = scratch[0][...] if scratch else x_ref[...]
        gate = jnp.dot(x, wg_ref[0].astype(BF16), preferred_element_type=F32)
        up = jnp.dot(x, wu_ref[0].astype(BF16), preferred_element_type=F32)
        act = (_silu(gate) * up).astype(BF16)
        o_ref[...] += jnp.dot(act, wd_ref[0].astype(BF16), preferred_element_type=F32)


def _ffn_call(x, w_gate, w_up, w_down, tile_expert, tile_src, n_valid, tm):
    tp, d = x.shape
    f = w_gate.shape[2]
    tf = FFN_TILE_F
    nf = f // tf
    nt = tp // tm

    def jeff(i, j, nv):
        return jnp.where(i < nv[0], j, nf - 1)

    grid_spec = pltpu.PrefetchScalarGridSpec(
        num_scalar_prefetch=3,
        grid=(nt, nf),
        in_specs=[
            pl.BlockSpec((tm, d), lambda i, j, te, xi, nv: (xi[i], 0)),
            pl.BlockSpec((1, d, tf), lambda i, j, te, xi, nv: (te[i], 0, jeff(i, j, nv))),
            pl.BlockSpec((1, d, tf), lambda i, j, te, xi, nv: (te[i], 0, jeff(i, j, nv))),
            pl.BlockSpec((1, tf, d), lambda i, j, te, xi, nv: (te[i], jeff(i, j, nv), 0)),
        ],
        out_specs=pl.BlockSpec((tm, d), lambda i, j, te, xi, nv: (i, 0)),
        scratch_shapes=[] if x.dtype == BF16 else [pltpu.VMEM((tm, d), BF16)],
    )
    return pl.pallas_call(
        _ffn_kernel, grid_spec=grid_spec, out_shape=jax.ShapeDtypeStruct((tp, d), F32),
        compiler_params=_cparams(("arbitrary", "arbitrary")), name="grouped_swiglu",
    )(tile_expert, tile_src, n_valid, x, w_gate, w_up, w_down)


def _dispatch_kernel(nrows_ref, idx_now, idx_next, table, o_ref, buf, sem, *, tm):
    i = pl.program_id(0)
    nrows = nrows_ref[0]
    slot = lax.rem(i, 2)
    valid = i * tm < nrows

    @pl.when((i == 0) & valid)
    def _():
        _issue_row_gathers(table, [idx_now], buf, sem, 0, tm)

    @pl.when((i + 1) * tm < nrows)
    def _():
        _issue_row_gathers(table, [idx_next], buf, sem, 1 - slot, tm)

    @pl.when(valid)
    def _():
        _wait_row_gathers(table, buf, sem, slot, 0, tm)
        o_ref[...] = buf[slot, 0].astype(o_ref.dtype)

    @pl.when(jnp.logical_not(valid))
    def _():
        o_ref[...] = jnp.zeros(o_ref.shape, o_ref.dtype)


def _dispatch_call(table, idx, out_dtype, n_rows):
    t_out = idx.shape[0]
    d = table.shape[1]
    tm = GATHER_TILE
    nt = t_out // tm
    idx3 = idx.reshape(nt, 1, tm)
    smem = functools.partial(pl.BlockSpec, memory_space=pltpu.SMEM)
    grid_spec = pltpu.PrefetchScalarGridSpec(
        num_scalar_prefetch=1, grid=(nt,),
        in_specs=[smem((1, 1, tm), lambda i, nr: (i, 0, 0)),
                  smem((1, 1, tm), lambda i, nr: (jnp.minimum(i + 1, nt - 1), 0, 0)),
                  pl.BlockSpec(memory_space=pl.ANY)],
        out_specs=pl.BlockSpec((tm, d), lambda i, nr: (i, 0)),
        scratch_shapes=[pltpu.VMEM((2, 1, tm, d), table.dtype), pltpu.SemaphoreType.DMA((2, 1))],
    )
    return pl.pallas_call(
        functools.partial(_dispatch_kernel, tm=tm),
        grid_spec=grid_spec, out_shape=jax.ShapeDtypeStruct((t_out, d), out_dtype),
        compiler_params=_cparams(("arbitrary",)), name="row_gather",
    )(jnp.asarray(n_rows, jnp.int32).reshape(1), idx3, idx3, table)


def _moe(h, sel, cw, w_gate, w_up, w_down):
    t, d = h.shape
    ne = sel.shape[0]
    tm = MOE_TILE_M
    hot = (sel > 0).astype(jnp.int32)
    rank = jnp.cumsum(hot, axis=1) - hot
    counts = jnp.sum(hot, axis=1)
    tiles_e = (counts + tm - 1) // tm
    tile_end = jnp.cumsum(tiles_e)
    tile_start = tile_end - tiles_e
    n_valid = tile_end[-1]
    n_tiles = (2 * t + tm - 1) // tm + ne
    tp = n_tiles * tm
    dest = tile_start[:, None] * tm + rank
    tid = jnp.arange(n_tiles, dtype=jnp.int32)
    te = jnp.sum((tid[:, None] >= tile_end[None, :]).astype(jnp.int32), axis=1)
    last = jnp.maximum(n_valid - 1, 0)
    te = jnp.where(tid < n_valid, te, te[last]).astype(jnp.int32)
    xi = jnp.minimum(tid, last).astype(jnp.int32)
    p1 = jnp.sum(jnp.where(sel == 1, dest, 0), axis=0)
    p2 = jnp.sum(jnp.where(sel == 2, dest, 0), axis=0)
    wts = jnp.stack([jnp.sum(jnp.where(sel == 1, cw, 0.0), axis=0),
                     jnp.sum(jnp.where(sel == 2, cw, 0.0), axis=0)], axis=1)

    pidx = jnp.stack([p1, p2]).astype(jnp.int32)
    tok = jnp.arange(t, dtype=jnp.int32)
    src = jnp.zeros((tp,), jnp.int32).at[pidx.reshape(-1)].set(jnp.concatenate([tok, tok]), unique_indices=True)
    xs = _dispatch_call(h, src, BF16, n_valid * tm)
    ys = _ffn_call(xs, w_gate, w_up, w_down, te, xi, n_valid.reshape(1).astype(jnp.int32), tm)
    return ys, pidx, wts


_SRC_Z_SSD = 0
_SRC_X_SSD = SSD_WIDTH
_SRC_BC = 2 * SSD_WIDTH
_SRC_DT = SSD_WIDTH + SSD_CONV_DIM
_SRC_QKV = _SRC_DT + SSD_HEADS
_SRC_Z_GDN = _SRC_QKV + GDN_CONV_DIM
_SRC_A = _SRC_Z_GDN + GDN_WIDTH
_SRC_B = _SRC_A + GDN_V_HEADS
_SEGMENTS = ((COL_QKV, _SRC_QKV, GDN_CONV_DIM), (COL_Z_SSD, _SRC_Z_SSD, SSD_WIDTH),
             (COL_Z_GDN, _SRC_Z_GDN, GDN_WIDTH), (COL_X_SSD, _SRC_X_SSD, SSD_WIDTH),
             (COL_BC, _SRC_BC, SSD_CONV_DIM - SSD_WIDTH), (COL_SMALL + SMALL_DT, _SRC_DT, SSD_HEADS),
             (COL_SMALL + SMALL_A, _SRC_A, 2 * GDN_V_HEADS))
_USED_COLS = COL_SMALL + SMALL_B + GDN_V_HEADS


def _reorder_kernel(w_ref, o_ref):
    for dst, src, rows in _SEGMENTS:
        o_ref[0, dst:dst + rows, :] = w_ref[0, src:src + rows, :].astype(BF16)
    o_ref[0, _USED_COLS:PROJ_COLS, :] = jnp.zeros((PROJ_COLS - _USED_COLS, o_ref.shape[-1]), BF16)


def _reorder_in_proj(w_in):
    depth, d, n_in = w_in.shape
    tk = 512
    return pl.pallas_call(
        _reorder_kernel, grid=(depth, d // tk),
        in_specs=[pl.BlockSpec((1, n_in, tk), lambda l, k: (l, 0, k))],
        out_specs=pl.BlockSpec((1, PROJ_COLS, tk), lambda l, k: (l, 0, k)),
        out_shape=jax.ShapeDtypeStruct((depth, PROJ_COLS, d), BF16),
        compiler_params=_cparams(("parallel", "parallel")), name="reorder_w_in",
    )(jnp.swapaxes(w_in, 1, 2))


def kernel(x_prompt, x_sample, c_prompt, c_sample, state_ssd_conv, state_ssm, state_gdn_conv, state_gdn, w_ada, b_ada, w_norm_mix, w_norm_ffn, w_in, w_conv_ssd, b_conv_ssd, ssd_dt_bias, ssd_a_log, ssd_d, ssd_norm, w_conv_gdn, gdn_dt_bias, gdn_a_log, gdn_norm, w_out, w_ffn_gate, w_ffn_up, w_ffn_down, w_router, w_exp_gate, w_exp_up, w_exp_down, w_norm_final):
    bp, lp, d = x_prompt.shape
    bs, ls, _ = x_sample.shape
    depth = w_in.shape[0]
    tpr = bp * lp
    tsa = bs * ls
    t = tpr + tsa

    x = (x_prompt.reshape(tpr, d), x_sample.reshape(tsa, d))
    mod = _ada_call(jnp.concatenate([c_sample, c_prompt], axis=0), w_ada, b_ada)
    w_in_t = _reorder_in_proj(w_in)
    w_out_b = w_out.astype(BF16)

    qp = PROMPT_CHUNK if lp % PROMPT_CHUNK == 0 else lp
    nseq_s = 16 // ls if ls < 16 else 1
    sw = GDN_V_HEADS * GDN_HEAD
    zeros_p = (jnp.zeros((1, bp, CONV_WIDTH - 1, SSD_CONV_DIM), F32), jnp.zeros((1, bp, SSD_WIDTH, SSD_STATE), F32),
               jnp.zeros((1, bp, CONV_WIDTH - 1, GDN_CONV_DIM), F32), jnp.zeros((1, bp, sw, GDN_HEAD), F32))
    ssm_in = state_ssm.reshape(depth, bs, SSD_WIDTH, SSD_STATE)
    gdn_in = state_gdn.reshape(depth, bs, sw, GDN_HEAD)

    new_p = [[], [], [], []]
    new_s_conv = [[], []]
    ssm_out = gdn_out = None
    common = dict(t_prompt=tpr, l_prompt=lp)
    mods = [(mod[l, bs:].reshape(bp, 1, N_MOD * d), jnp.repeat(mod[l, :bs], ls, axis=0)) for l in range(depth)]
    h = _ew_call(x, w_norm_mix[0], mod=(1, 0) + mods[0], **common)
    for l in range(depth):
        mod_p, mod_s = mods[l]
        lw = {"w_conv_ssd": w_conv_ssd[l], "b_conv_ssd": b_conv_ssd[l], "ssd_dt_bias": ssd_dt_bias[l],
              "ssd_a_log": ssd_a_log[l], "ssd_d": ssd_d[l], "ssd_norm": ssd_norm[l],
              "w_conv_gdn": w_conv_gdn[l], "gdn_dt_bias": gdn_dt_bias[l], "gdn_a_log": gdn_a_log[l],
              "gdn_norm": gdn_norm[l]}
        proj = _in_proj_call(h, w_in_t, l)

        prompt = dict(row0=0, bsz=bp, seq=lp, q=qp, nseq=1)
        sample = dict(row0=tpr, bsz=bs, seq=ls, q=ls, nseq=nseq_s)
        yp_ssd, pc, pst = _ssd_call(proj, lw, zeros_p[0], zeros_p[1], 0, **prompt)
        ys_ssd, sc_, ssm_out = _ssd_call(proj, lw, state_ssd_conv, ssm_in, l, stacked=(depth, ssm_out), **sample)
        yp_gdn, pgc, pgst = _gdn_call(proj, lw, zeros_p[2], zeros_p[3], 0, **prompt)
        ys_gdn, sgc, gdn_out = _gdn_call(proj, lw, state_gdn_conv, gdn_in, l, stacked=(depth, gdn_out), **sample)
        for lst, v in zip(new_p, (pc, pst, pgc, pgst)):
            lst.append(v)
        new_s_conv[0].append(sc_)
        new_s_conv[1].append(sgc)

        mix = _out_proj_call(yp_ssd, yp_gdn, ys_ssd, ys_gdn, w_out_b[l])
        moe_layer = l % 2 == 1
        i = l // 2
        if moe_layer:
            x, h2, sel, cw = _ew_call(x, w_norm_ffn[l], res=(mix, 2, mod_p, mod_s), mod=(4, 3, mod_p, mod_s),
                                      out_x=True, h_dtype=F32, router_w=w_router[i], **common)
            f = _moe(h2, sel, cw, w_exp_gate[i], w_exp_up[i], w_exp_down[i])
        else:
            x, h2 = _ew_call(x, w_norm_ffn[l], res=(mix, 2, mod_p, mod_s), mod=(4, 3, mod_p, mod_s), out_x=True,
                             **common)
            nt = t // FFN_TILE_M
            tid = jnp.arange(nt, dtype=jnp.int32)
            f = _ffn_call(h2, w_ffn_gate[i][None], w_ffn_up[i][None], w_ffn_down[i][None],
                          jnp.zeros((nt,), jnp.int32), tid, jnp.full((1,), nt, jnp.int32), FFN_TILE_M)
        if l + 1 < depth:
            x, h = _ew_call(x, w_norm_mix[l + 1], res=(f, 5, mod_p, mod_s), mod=(1, 0) + mods[l + 1],
                            out_x=True, **common)
        else:
            y_p, y_s = _ew_call(x, w_norm_final, res=(f, 5, mod_p, mod_s), h_dtype=F32, split_out=True, **common)

    ssm_shape = (SSD_HEADS, SSD_HEAD_DIM, SSD_STATE)
    gdn_shape = (GDN_V_HEADS, GDN_HEAD, GDN_HEAD)
    prompt_states = (jnp.stack(new_p[0]), jnp.stack(new_p[1]).reshape((depth, bp) + ssm_shape),
                     jnp.stack(new_p[2]), jnp.stack(new_p[3]).reshape((depth, bp) + gdn_shape))
    sample_states = (jnp.stack(new_s_conv[0]), ssm_out.reshape((depth, bs) + ssm_shape),
                     jnp.stack(new_s_conv[1]), gdn_out.reshape((depth, bs) + gdn_shape))
    return (y_p.reshape(bp, lp, d), y_s.reshape(bs, ls, d)) + prompt_states + sample_states
```

```python
import functools

import jax
import jax.numpy as jnp
from jax import lax
from jax.experimental import pallas as pl
from jax.experimental.pallas import tpu as pltpu

F32 = jnp.float32
BF16 = jnp.bfloat16

D_MODEL = 2048
CONV_WIDTH = 4
SSD_WIDTH = 1024
SSD_HEAD_DIM = 64
SSD_HEADS = 16
SSD_GROUPS = 2
SSD_STATE = 128
SSD_CONV_DIM = SSD_WIDTH + 2 * SSD_GROUPS * SSD_STATE
GDN_WIDTH = 1024
GDN_HEAD = 128
GDN_V_HEADS = 8
GDN_K_HEADS = 4
GDN_QK_DIM = GDN_K_HEADS * GDN_HEAD
GDN_CONV_DIM = 2 * GDN_QK_DIM + GDN_WIDTH
N_EXPERTS = 8
N_MOD = 6
EPS = 1e-6

PROJ_COLS = 5760
COL_QKV, COL_Z_SSD, COL_Z_GDN, COL_X_SSD, COL_BC, COL_SMALL = 0, 2048, 3072, 4096, 5120, 5632
SMALL_DT, SMALL_A, SMALL_B = 0, SSD_HEADS, SSD_HEADS + GDN_V_HEADS

V7X_VMEM_LIMIT = 56 * 1024 * 1024
TOKEN_TILE = 256
MM_TILE_M = 1024
IN_PROJ_TILE_N = 1920
FFN_TILE_M = 768
MOE_TILE_M = 768
FFN_TILE_F = 512
GATHER_TILE = 256
GATHER_UNROLL = 8
PROMPT_CHUNK = 128
SAMPLE_ROWS_PER_STEP = 64
NEG_BIG = -1e30


def _cparams(sem):
    return pltpu.CompilerParams(dimension_semantics=sem, vmem_limit_bytes=V7X_VMEM_LIMIT)


_NN = (((1,), (0,)), ((), ()))
_NT = (((1,), (1,)), ((), ()))
_TN = (((0,), (0,)), ((), ()))


def _dg(a, b, dims):
    return lax.dot_general(a, b, dims, preferred_element_type=F32)


def _dot(a, b, dims=_NN):
    return _dg(a.astype(BF16), b.astype(BF16), dims)


def _split(x, n):
    parts = []
    r = x
    for _ in range(n - 1):
        p = r.astype(BF16)
        parts.append(p)
        r = r - p.astype(F32)
    parts.append(r.astype(BF16))
    return parts


def _dot_exact_rhs(a, b, dims=_NN):
    bb = b.astype(BF16)
    a1, a2, a3 = _split(a, 3)
    return _dg(a3, bb, dims) + _dg(a2, bb, dims) + _dg(a1, bb, dims)


def _dot_exact_lhs(a, b, dims=_NN):
    ab = a.astype(BF16)
    b1, b2, b3 = _split(b, 3)
    return _dg(ab, b3, dims) + _dg(ab, b2, dims) + _dg(ab, b1, dims)


def _dot6(a, b, dims=_NN):
    a1, a2, a3 = _split(a, 3)
    b1, b2, b3 = _split(b, 3)
    small = _dg(a1, b3, dims) + _dg(a2, b2, dims) + _dg(a3, b1, dims)
    mid = _dg(a1, b2, dims) + _dg(a2, b1, dims)
    return small + mid + _dg(a1, b1, dims)


def _sigmoid(x):
    return 1.0 / (1.0 + jnp.exp(-x))


def _silu(x):
    return x * _sigmoid(x)


def _softplus(x):
    return jnp.maximum(x, 0.0) + jnp.log1p(jnp.exp(-jnp.abs(x)))


def _iota(shape, dim):
    return lax.broadcasted_iota(jnp.int32, shape, dim)


def _transpose_cols(x, n):
    eye = (_iota((n, n), 0) == _iota((n, n), 1)).astype(F32)
    return _dot_exact_lhs(eye, x, _NT)


def _conv_silu(u, ext_ref, s, w_ref, bias, q):
    ext_ref[s, 8:8 + q, :] = u
    acc = u * w_ref[CONV_WIDTH - 1:CONV_WIDTH, :]
    if bias is not None:
        acc = acc + bias
    for j in range(1, CONV_WIDTH):
        acc = acc + ext_ref[s, 8 - j:8 - j + q, :] * w_ref[CONV_WIDTH - 1 - j:CONV_WIDTH - j, :]
    return _silu(acc)


def _conv_advance(ext_ref, s, q):
    ext_ref[s, 0:8, :] = ext_ref[s, q:q + 8, :]


def _conv_init(ext_ref, s, state_rows):
    ext_ref[s, 0:8, :] = jnp.zeros((8, ext_ref.shape[-1]), F32)
    ext_ref[s, 8 - (CONV_WIDTH - 1):8, :] = state_rows


def _conv_tail(ext_ref, s, q):
    return ext_ref[s, q + 8 - (CONV_WIDTH - 1):q + 8, :]


def _ada_kernel(c_ref, w_ref, b_ref, o_ref):
    a = _silu(c_ref[...]).astype(BF16)
    o_ref[0] = jnp.dot(a, w_ref[0].astype(BF16), preferred_element_type=F32) + b_ref[0]


def _ada_call(c_all, w_ada, b_ada):
    depth, d, n = w_ada.shape
    m = c_all.shape[0]
    tn = 1024
    return pl.pallas_call(
        _ada_kernel,
        grid=(depth, n // tn),
        in_specs=[
            pl.BlockSpec((m, d), lambda l, j: (0, 0)),
            pl.BlockSpec((1, d, tn), lambda l, j: (l, 0, j)),
            pl.BlockSpec((1, 1, tn), lambda l, j: (l, 0, j)),
        ],
        out_specs=pl.BlockSpec((1, m, tn), lambda l, j: (l, 0, j)),
        out_shape=jax.ShapeDtypeStruct((depth, m, n), F32),
        compiler_params=_cparams(("parallel", "parallel")),
        name="ada_mod",
    )(c_all, w_ada, b_ada.reshape(depth, 1, n))


def _issue_row_gathers(table, idx_refs, buf, sem, slot, tm):
    for k, idx_ref in enumerate(idx_refs):
        def body(r8, carry, k=k, idx_ref=idx_ref):
            for u in range(GATHER_UNROLL):
                r = r8 * GATHER_UNROLL + u
                pltpu.make_async_copy(table.at[pl.ds(idx_ref[0, 0, r], 1), :],
                                      buf.at[slot, k, pl.ds(r, 1), :], sem.at[slot, k]).start()
            return carry
        lax.fori_loop(0, tm // GATHER_UNROLL, body, 0)


def _wait_row_gathers(table, buf, sem, slot, k, tm):
    pltpu.make_async_copy(table.at[pl.ds(0, tm), :], buf.at[slot, k], sem.at[slot, k]).wait()


def _route_top2(h, wr_ref, sel_ref, cw_ref):
    logits = _dot6(wr_ref[...], h, _NT)
    ne = logits.shape[0]
    ei = _iota(logits.shape, 0)
    m1 = jnp.max(logits, axis=0, keepdims=True)
    i1 = jnp.min(jnp.where(logits == m1, ei, ne), axis=0, keepdims=True)
    rest = jnp.where(ei == i1, -jnp.inf, logits)
    m2 = jnp.max(rest, axis=0, keepdims=True)
    i2 = jnp.min(jnp.where(rest == m2, ei, ne), axis=0, keepdims=True)
    e = jnp.exp(m2 - m1)
    w1 = 1.0 / (1.0 + e)
    w2 = e / (1.0 + e)
    sel_ref[...] = jnp.where(ei == i1, 1, jnp.where(ei == i2, 2, 0)).astype(jnp.int32)
    cw_ref[...] = jnp.where(ei == i1, w1, jnp.where(ei == i2, w2, 0.0))


def _ew_kernel(*refs, split_in, has_res, n_gather, modded, routed, out_x, split_out, n_prompt_tiles):
    refs = list(refs)
    xp_ref = refs.pop(0)
    xs_ref = refs.pop(0) if split_in else xp_ref
    if n_gather:
        idx_now = [refs.pop(0) for _ in range(n_gather)]
        idx_next = [refs.pop(0) for _ in range(n_gather)]
        table, wts_ref = refs.pop(0), refs.pop(0)
    elif has_res:
        y_ref = refs.pop(0)
    if has_res:
        gp_ref, gs_ref = refs.pop(0), refs.pop(0)
    w_ref = refs.pop(0)
    if modded:
        scp_ref, shp_ref, scs_ref, shs_ref = refs.pop(0), refs.pop(0), refs.pop(0), refs.pop(0)
    if routed:
        wr_ref = refs.pop(0)
    if out_x:
        xo_ref = refs.pop(0)
    hp_ref = refs.pop(0)
    hs_ref = refs.pop(0) if split_out else hp_ref
    if routed:
        sel_ref, cw_ref = refs.pop(0), refs.pop(0)
    i = pl.program_id(0)

    y = None
    if n_gather:
        buf, sem = refs.pop(0), refs.pop(0)
        tm = buf.shape[2]
        slot = lax.rem(i, 2)

        @pl.when(i == 0)
        def _():
            _issue_row_gathers(table, idx_now, buf, sem, 0, tm)

        @pl.when(i + 1 < pl.num_programs(0))
        def _():
            _issue_row_gathers(table, idx_next, buf, sem, 1 - slot, tm)

        for k in range(n_gather):
            _wait_row_gathers(table, buf, sem, slot, k, tm)
            term = buf[slot, k] * wts_ref[:, k:k + 1]
            y = term if y is None else y + term
    elif has_res:
        y = y_ref[...]

    def run(prompt):
        x = (xp_ref if prompt else xs_ref)[...]
        if has_res:
            g = gp_ref[0] if prompt else gs_ref[...]
            x = x + g * y
            if out_x:
                xo_ref[...] = x
        h = x * lax.rsqrt(jnp.mean(x * x, axis=-1, keepdims=True) + EPS) * w_ref[...]
        if modded:
            sc = scp_ref[0] if prompt else scs_ref[...]
            sh = shp_ref[0] if prompt else shs_ref[...]
            h = h * (1.0 + sc) + sh
        h_ref = hp_ref if prompt else hs_ref
        h_ref[...] = h.astype(h_ref.dtype)
        if routed:
            _route_top2(h, wr_ref, sel_ref, cw_ref)

    @pl.when(i < n_prompt_tiles)
    def _():
        run(True)

    @pl.when(i >= n_prompt_tiles)
    def _():
        run(False)


def _ew_call(x, w, *, t_prompt, l_prompt, res=None, mod=None, out_x=False, h_dtype=BF16, split_out=False,
             router_w=None):
    split_in = isinstance(x, tuple)
    d = w.shape[0]
    t = sum(a.shape[0] for a in x) if split_in else x.shape[0]
    tm = TOKEN_TILE
    npt = t_prompt // tm
    per_seq = l_prompt // tm

    def pspec(chunk, mp):
        bp = mp.shape[0]
        return pl.BlockSpec((1, 1, d), lambda i: (jnp.minimum(i // per_seq, bp - 1), 0, chunk))

    def sspec(chunk):
        return pl.BlockSpec((tm, d), lambda i: (jnp.maximum(i - npt, 0), chunk))

    row = pl.BlockSpec((tm, d), lambda i: (i, 0))
    prow = pl.BlockSpec((tm, d), lambda i: (jnp.minimum(i, npt - 1), 0))
    srow = sspec(0)
    args, specs = (list(x), [prow, srow]) if split_in else ([x], [row])
    n_gather = 0
    scratch = []
    if res is not None:
        y, gch, gmp, gms = res
        if isinstance(y, tuple):
            table, idx, wts = y
            n_gather = idx.shape[0]
            nt = t // tm
            idx3 = idx.reshape(n_gather, nt, 1, tm)
            smem = functools.partial(pl.BlockSpec, memory_space=pltpu.SMEM)
            args += [idx3[k] for k in range(n_gather)] * 2 + [table, wts]
            specs += ([smem((1, 1, tm), lambda i: (i, 0, 0)) for _ in range(n_gather)]
                      + [smem((1, 1, tm), lambda i: (jnp.minimum(i + 1, nt - 1), 0, 0)) for _ in range(n_gather)]
                      + [pl.BlockSpec(memory_space=pl.ANY), pl.BlockSpec((tm, n_gather), lambda i: (i, 0))])
            scratch = [pltpu.VMEM((2, n_gather, tm, d), table.dtype), pltpu.SemaphoreType.DMA((2, n_gather))]
        else:
            args.append(y)
            specs.append(row)
        args += [gmp, gms]
        specs += [pspec(gch, gmp), sspec(gch)]
    args.append(w.reshape(1, d))
    specs.append(pl.BlockSpec((1, d), lambda i: (0, 0)))
    if mod is not None:
        scc, shc, mmp, mms = mod
        args += [mmp, mmp, mms, mms]
        specs += [pspec(scc, mmp), pspec(shc, mmp), sspec(scc), sspec(shc)]
    if router_w is not None:
        ne = router_w.shape[1]
        args.append(router_w.T)
        specs.append(pl.BlockSpec((ne, d), lambda i: (0, 0)))
    out_shape, out_specs = [], []
    if out_x:
        out_shape.append(jax.ShapeDtypeStruct((t, d), F32))
        out_specs.append(row)
    if split_out:
        out_shape += [jax.ShapeDtypeStruct((t_prompt, d), h_dtype), jax.ShapeDtypeStruct((t - t_prompt, d), h_dtype)]
        out_specs += [prow, srow]
    else:
        out_shape.append(jax.ShapeDtypeStruct((t, d), h_dtype))
        out_specs.append(row)
    if router_w is not None:
        out_shape += [jax.ShapeDtypeStruct((ne, t), jnp.int32), jax.ShapeDtypeStruct((ne, t), F32)]
        out_specs += [pl.BlockSpec((ne, tm), lambda i: (0, i)), pl.BlockSpec((ne, tm), lambda i: (0, i))]
    kern = functools.partial(_ew_kernel, split_in=split_in, has_res=res is not None, n_gather=n_gather,
                             modded=mod is not None, routed=router_w is not None, out_x=out_x,
                             split_out=split_out, n_prompt_tiles=npt)
    outs = pl.pallas_call(
        kern, grid=(t // tm,), in_specs=specs, out_specs=out_specs, out_shape=out_shape,
        scratch_shapes=scratch,
        compiler_params=_cparams(("arbitrary",)), name="token_norm",
    )(*args)
    return outs if len(outs) > 1 else outs[0]


def _in_proj_kernel(a_ref, wt_ref, o_ref):
    o_ref[...] = _dg(a_ref[...], wt_ref[...], _NT)


def _in_proj_call(a, w_t, layer):
    m, k = a.shape
    n = w_t.shape[1]
    tn = IN_PROJ_TILE_N
    tm = MM_TILE_M
    while m % tm:
        tm //= 2
    return pl.pallas_call(
        _in_proj_kernel,
        grid=(m // tm, n // tn),
        in_specs=[pl.BlockSpec((tm, k), lambda i, j: (i, 0)),
                  pl.BlockSpec((None, tn, k), lambda i, j: (layer, j, 0))],
        out_specs=pl.BlockSpec((tm, tn), lambda i, j: (i, j)),
        out_shape=jax.ShapeDtypeStruct((m, n), F32),
        compiler_params=_cparams(("parallel", "parallel")),
        name="in_proj",
    )(a, w_t)


def _out_proj_kernel(ap1_ref, ap2_ref, as1_ref, as2_ref, w1_ref, w2_ref, o_ref, *, n_prompt_tiles):
    i = pl.program_id(0)

    def run(a1_ref, a2_ref):
        o_ref[...] = (jnp.dot(a1_ref[...], w1_ref[...], preferred_element_type=F32)
                      + jnp.dot(a2_ref[...], w2_ref[...], preferred_element_type=F32))

    @pl.when(i < n_prompt_tiles)
    def _():
        run(ap1_ref, ap2_ref)

    @pl.when(i >= n_prompt_tiles)
    def _():
        run(as1_ref, as2_ref)


def _out_proj_call(yp_ssd, yp_gdn, ys_ssd, ys_gdn, w_out):
    tp, k1 = yp_ssd.shape
    ts = ys_ssd.shape[0]
    k2 = yp_gdn.shape[1]
    n = w_out.shape[1]
    tm = MM_TILE_M
    while tp % tm or ts % tm:
        tm //= 2
    tn = 1024
    npt = tp // tm
    pmap = lambda i, j: (jnp.minimum(i, npt - 1), 0)
    smap = lambda i, j: (jnp.maximum(i - npt, 0), 0)
    return pl.pallas_call(
        functools.partial(_out_proj_kernel, n_prompt_tiles=npt),
        grid=((tp + ts) // tm, n // tn),
        in_specs=[pl.BlockSpec((tm, k1), pmap), pl.BlockSpec((tm, k2), pmap),
                  pl.BlockSpec((tm, k1), smap), pl.BlockSpec((tm, k2), smap),
                  pl.BlockSpec((k1, tn), lambda i, j: (0, j)),
                  pl.BlockSpec((k2, tn), lambda i, j: (k1 // k2, j))],
        out_specs=pl.BlockSpec((tm, tn), lambda i, j: (i, j)),
        out_shape=jax.ShapeDtypeStruct((tp + ts, n), F32),
        compiler_params=_cparams(("parallel", "parallel")),
        name="out_proj",
    )(yp_ssd, yp_gdn, ys_ssd, ys_gdn, w_out, w_out)


def _store_state(sto_ref, s, value, stack):
    if stack == "first":
        sto_ref[0, s] = value
        for l in range(1, sto_ref.shape[0]):
            sto_ref[l, s] = jnp.zeros(value.shape, value.dtype)
    else:
        sto_ref[s] = value


def _ssd_kernel(*refs, q, nseq, nc, stack):
    (z_ref, xs_ref, bc_ref, sm_ref, wx_ref, wbc_ref, bx_ref, bbc_ref, dtb_ref, alog_ref,
     dexp_ref, nw_ref, cst_ref, st_ref) = refs[:14]
    y_ref, cso_ref, sto_ref, extx, extbc, s_scr = refs[14 + (stack == "next"):]
    c = pl.program_id(1)
    nh = SSD_HEADS
    hg = SSD_HEADS // SSD_GROUPS
    wg = SSD_WIDTH // SSD_GROUPS
    p = SSD_HEAD_DIM
    n = SSD_STATE
    gn = SSD_GROUPS * n

    li = _iota((q, q), 0)
    si = _iota((q, q), 1)
    tril = si <= li
    tri = tril.astype(F32)
    expand = (_iota((nh, SSD_WIDTH), 1) // p == _iota((nh, SSD_WIDTH), 0)).astype(F32)
    lane = _iota((q, 128), 1)

    for s in range(nseq):
        rows = slice(s * q, (s + 1) * q)

        @pl.when(c == 0)
        def _():
            _conv_init(extx, s, cst_ref[s, :, 0:SSD_WIDTH])
            _conv_init(extbc, s, cst_ref[s, :, SSD_WIDTH:SSD_CONV_DIM])
            s_scr[s] = st_ref[s]

        xs = _conv_silu(xs_ref[rows, :], extx, s, wx_ref, bx_ref[...], q)
        bcv = _conv_silu(bc_ref[rows, :], extbc, s, wbc_ref, bbc_ref[...], q)

        @pl.when(c == nc - 1)
        def _():
            cso_ref[s, :, 0:SSD_WIDTH] = _conv_tail(extx, s, q)
            cso_ref[s, :, SSD_WIDTH:SSD_CONV_DIM] = _conv_tail(extbc, s, q)

        _conv_advance(extx, s, q)
        _conv_advance(extbc, s, q)

        dt = _softplus(sm_ref[rows, :][:, SMALL_DT:SMALL_DT + nh] + dtb_ref[...])
        a = dt * (-jnp.exp(alog_ref[...]))
        a_cum = _dot_exact_lhs(tri, a)
        a_cum_t = _transpose_cols(a_cum, nh)
        dt_e = _dot_exact_rhs(dt, expand)
        acum_e = _dot_exact_rhs(a_cum, expand)
        xdt = xs * dt_e
        alast_e = acum_e[q - 1:q, :]
        xd = xdt * jnp.exp(alast_e - acum_e)
        dec_last = jnp.exp(a_cum_t[:, q - 1:q])

        pieces = []
        for g in range(SSD_GROUPS):
            bm = bcv[:, g * n:(g + 1) * n]
            cm = bcv[:, gn + g * n:gn + (g + 1) * n]
            cb = _dot(cm, bm, _NT)
            s_old = s_scr[s, g * wg:(g + 1) * wg, :]
            pieces.append(_dot(cm, s_old, _NT))
            decayed = jnp.concatenate(
                [s_old[hl * p:(hl + 1) * p, :] * dec_last[g * hg + hl:g * hg + hl + 1, :] for hl in range(hg)], axis=0)
            s_scr[s, g * wg:(g + 1) * wg, :] = decayed + _dot(xd[:, g * wg:(g + 1) * wg], bm, _TN)
            for k in range(hg // 2):
                ms = []
                for hh in (g * hg + 2 * k, g * hg + 2 * k + 1):
                    seg = a_cum[:, hh:hh + 1] - a_cum_t[hh:hh + 1, :]
                    ms.append(cb * jnp.exp(jnp.where(tril, seg, NEG_BIG)))
                c0 = g * wg + 2 * p * k
                xp = xdt[:, c0:c0 + 2 * p]
                top = jnp.where(lane < p, xp, 0.0)
                bot = jnp.where(lane >= p, xp, 0.0)
                if q % 128 == 0:
                    piece = _dot(jnp.concatenate(ms, axis=1), jnp.concatenate([top, bot], axis=0))
                else:
                    piece = _dot(ms[0], top) + _dot(ms[1], bot)
                pieces.append(piece)
        y_off = jnp.concatenate([pieces[0], pieces[1 + hg // 2]], axis=1) * jnp.exp(acum_e)
        y_diag = jnp.concatenate(pieces[1:1 + hg // 2] + pieces[2 + hg // 2:], axis=1)
        y = y_off + y_diag + dexp_ref[...] * xs
        y = y * _silu(z_ref[rows, :])
        outs = []
        for g in range(SSD_GROUPS):
            yg = y[:, g * wg:(g + 1) * wg]
            outs.append(yg * lax.rsqrt(jnp.mean(yg * yg, axis=-1, keepdims=True) + EPS))
        y_ref[rows, :] = (jnp.concatenate(outs, axis=1) * nw_ref[...]).astype(y_ref.dtype)

        @pl.when(c == nc - 1)
        def _():
            _store_state(sto_ref, s, s_scr[s], stack)


def _state_out(stacked, layer, bsz, nseq, rows, n, in_specs, args):
    if stacked is None:
        return (jax.ShapeDtypeStruct((bsz, rows, n), F32),
                pl.BlockSpec((nseq, rows, n), lambda b, c: (b, 0, 0)), None, {})
    depth, prev = stacked
    shape = jax.ShapeDtypeStruct((depth, bsz, rows, n), F32)
    if prev is None:
        return shape, pl.BlockSpec((depth, nseq, rows, n), lambda b, c: (0, b, 0, 0)), "first", {}
    in_specs.append(pl.BlockSpec(memory_space=pl.ANY))
    args.append(prev)
    return (shape, pl.BlockSpec((None, nseq, rows, n), lambda b, c: (layer, b, 0, 0)), "next",
            {len(args) - 1: 2})


def _ssd_call(proj, lw, conv_state, ssm_state, layer, *, row0, bsz, seq, q, nseq, stacked=None):
    nc = seq // q
    r = nseq * q
    assert nseq == 1 or nc == 1
    rb0 = row0 // r
    n = SSD_STATE
    bcw = SSD_CONV_DIM - SSD_WIDTH
    grid = (bsz // nseq, nc)
    rowi = lambda b, c: rb0 + b * nc + c
    full = lambda shape: pl.BlockSpec(shape, lambda b, c: (0,) * len(shape))

    in_specs = [
        pl.BlockSpec((r, SSD_WIDTH), lambda b, c: (rowi(b, c), COL_Z_SSD // SSD_WIDTH)),
        pl.BlockSpec((r, SSD_WIDTH), lambda b, c: (rowi(b, c), COL_X_SSD // SSD_WIDTH)),
        pl.BlockSpec((r, bcw), lambda b, c: (rowi(b, c), COL_BC // bcw)),
        pl.BlockSpec((r, 128), lambda b, c: (rowi(b, c), COL_SMALL // 128)),
        pl.BlockSpec((CONV_WIDTH, SSD_WIDTH), lambda b, c: (0, 0)),
        pl.BlockSpec((CONV_WIDTH, bcw), lambda b, c: (0, SSD_WIDTH // bcw)),
        pl.BlockSpec((1, SSD_WIDTH), lambda b, c: (0, 0)),
        pl.BlockSpec((1, bcw), lambda b, c: (0, SSD_WIDTH // bcw)),
        full((1, SSD_HEADS)), full((1, SSD_HEADS)), full((1, SSD_WIDTH)), full((1, SSD_WIDTH)),
        pl.BlockSpec((None, nseq, CONV_WIDTH - 1, SSD_CONV_DIM), lambda b, c: (layer, b, 0, 0)),
        pl.BlockSpec((None, nseq, SSD_WIDTH, n), lambda b, c: (layer, b, 0, 0)),
    ]
    cw = lw["w_conv_ssd"]
    cbias = lw["b_conv_ssd"].reshape(1, SSD_CONV_DIM)
    args = [proj, proj, proj, proj, cw, cw, cbias, cbias,
            lw["ssd_dt_bias"].reshape(1, SSD_HEADS), lw["ssd_a_log"].reshape(1, SSD_HEADS),
            jnp.repeat(lw["ssd_d"], SSD_HEAD_DIM).reshape(1, SSD_WIDTH), lw["ssd_norm"].reshape(1, SSD_WIDTH),
            conv_state, ssm_state]
    st_shape, st_spec, stack, aliases = _state_out(stacked, layer, bsz, nseq, SSD_WIDTH, n, in_specs, args)
    out_shape = [
        jax.ShapeDtypeStruct((bsz * seq, SSD_WIDTH), BF16),
        jax.ShapeDtypeStruct((bsz, CONV_WIDTH - 1, SSD_CONV_DIM), F32),
        st_shape,
    ]
    out_specs = [
        pl.BlockSpec((r, SSD_WIDTH), lambda b, c: (b * nc + c, 0)),
        pl.BlockSpec((nseq, CONV_WIDTH - 1, SSD_CONV_DIM), lambda b, c: (b, 0, 0)),
        st_spec,
    ]
    scratch = [pltpu.VMEM((nseq, q + 8, SSD_WIDTH), F32), pltpu.VMEM((nseq, q + 8, bcw), F32),
               pltpu.VMEM((nseq, SSD_WIDTH, n), F32)]
    kern = functools.partial(_ssd_kernel, q=q, nseq=nseq, nc=nc, stack=stack)
    return pl.pallas_call(
        kern, grid=grid, in_specs=in_specs, out_specs=out_specs, out_shape=out_shape,
        scratch_shapes=scratch, input_output_aliases=aliases,
        compiler_params=_cparams(("parallel", "arbitrary")), name="ssd_mixer",
    )(*args)


def _neumann_inverse(xs, nil):
    n = xs[0].shape[0]
    eye = (_iota((n, n), 0) == _iota((n, n), 1)).astype(F32)
    invs = [eye - x for x in xs]
    pws = xs
    k = 2
    while k < nil:
        pws = [_dot(pw, pw) for pw in pws]
        invs = [inv + _dot(inv, pw) for inv, pw in zip(invs, pws)]
        k *= 2
    return invs


def _unit_lower_inverse(ms, q):
    blk = 16
    if q <= blk:
        return _neumann_inverse(ms, q)
    same = (_iota((q, q), 0) // blk) == (_iota((q, q), 1) // blk)
    dgs = [jnp.where(same, m, 0.0) for m in ms]
    dinvs = _neumann_inverse(dgs, blk)
    nns = [_dot(dinv, m - dg) for dinv, m, dg in zip(dinvs, ms, dgs)]
    ninvs = _neumann_inverse(nns, q // blk)
    return [_dot(ninv, dinv) for ninv, dinv in zip(ninvs, dinvs)]


def _gdn_kernel(*refs, q, nseq, nc, stack):
    qkv_ref, z_ref, sm_ref, w_ref, dtb_ref, alog_ref, nw_ref, cst_ref, st_ref = refs[:9]
    y_ref, cso_ref, sto_ref, ext, s_scr = refs[9 + (stack == "next"):]
    c = pl.program_id(1)
    nv = GDN_V_HEADS
    rep = GDN_V_HEADS // GDN_K_HEADS
    dk = GDN_HEAD
    dv = GDN_HEAD

    li = _iota((q, q), 0)
    si = _iota((q, q), 1)
    tril = si <= li
    strict = si < li
    tri = tril.astype(F32)

    heads = []
    for s in range(nseq):
        rows = slice(s * q, (s + 1) * q)

        @pl.when(c == 0)
        def _():
            _conv_init(ext, s, cst_ref[s])
            s_scr[s] = st_ref[s]

        xc = _conv_silu(qkv_ref[rows, :], ext, s, w_ref, None, q)

        @pl.when(c == nc - 1)
        def _():
            cso_ref[s] = _conv_tail(ext, s, q)

        _conv_advance(ext, s, q)

        sm = sm_ref[rows, :]
        a_raw = sm[:, SMALL_A:SMALL_A + nv]
        b_raw = sm[:, SMALL_B:SMALL_B + nv]
        gate = -jnp.exp(alog_ref[...]) * _softplus(a_raw + dtb_ref[...])
        g_cum = _dot_exact_lhs(tri, gate)
        g_cum_t = _transpose_cols(g_cum, nv)
        beta_all = _sigmoid(b_raw)

        for kh in range(GDN_K_HEADS):
            qc = xc[:, kh * dk:(kh + 1) * dk]
            kc = xc[:, GDN_QK_DIM + kh * dk:GDN_QK_DIM + (kh + 1) * dk]
            qn = qc * lax.rsqrt(jnp.sum(qc * qc, axis=-1, keepdims=True) + EPS) * (dk ** -0.5)
            kn = kc * lax.rsqrt(jnp.sum(kc * kc, axis=-1, keepdims=True) + EPS)
            kk = _dot(kn, kn, _NT)
            qk = _dot(qn, kn, _NT)
            for j in range(kh * rep, (kh + 1) * rep):
                col = g_cum[:, j:j + 1]
                seg = col - g_cum_t[j:j + 1, :]
                decay = jnp.exp(jnp.where(tril, seg, NEG_BIG))
                beta = beta_all[:, j:j + 1]
                e_col = jnp.exp(col)
                g_last = g_cum[q - 1:q, j:j + 1]
                vj = xc[:, 2 * GDN_QK_DIM + j * dv:2 * GDN_QK_DIM + (j + 1) * dv]
                heads.append(dict(
                    s=s, j=j, rows=rows,
                    m=jnp.where(strict, beta * kk * decay, 0.0),
                    rhs=jnp.concatenate([vj * beta, kn * (beta * e_col)], axis=1),
                    attn=qk * decay, q_dec=qn * e_col, k_dec=kn * jnp.exp(g_last - col),
                    s_dec=jnp.exp(g_last)))

    t_invs = _unit_lower_inverse([hd["m"] for hd in heads], q)
    sols = [_dot(t_inv, hd["rhs"]) for t_inv, hd in zip(t_invs, heads)]
    states = [s_scr[hd["s"], dk * hd["j"]:dk * (hd["j"] + 1), :] for hd in heads]
    boths = [_dot(jnp.concatenate([sol[:, dv:], hd["q_dec"]], axis=0), st)
             for sol, hd, st in zip(sols, heads, states)]
    v_news = [sol[:, :dv] - both[:q] for sol, both in zip(sols, boths)]
    outs = [both[q:] + _dot(hd["attn"], v_new) for both, hd, v_new in zip(boths, heads, v_news)]
    for hd, st, v_new in zip(heads, states, v_news):
        s_scr[hd["s"], dk * hd["j"]:dk * (hd["j"] + 1), :] = st * hd["s_dec"] + _dot(hd["k_dec"], v_new, _TN)
    for s in range(nseq):
        rows = slice(s * q, (s + 1) * q)
        ys = []
        for hd, o in zip(heads, outs):
            if hd["s"] == s:
                o = o * lax.rsqrt(jnp.mean(o * o, axis=-1, keepdims=True) + EPS) * nw_ref[...]
                ys.append(o * _silu(z_ref[rows, dv * hd["j"]:dv * (hd["j"] + 1)]))
        y_ref[rows, :] = jnp.concatenate(ys, axis=1).astype(y_ref.dtype)

        @pl.when(c == nc - 1)
        def _():
            _store_state(sto_ref, s, s_scr[s], stack)


def _gdn_call(proj, lw, conv_state, gdn_state, layer, *, row0, bsz, seq, q, nseq, stacked=None):
    nc = seq // q
    r = nseq * q
    assert nseq == 1 or nc == 1
    rb0 = row0 // r
    hd = GDN_HEAD
    sw = GDN_V_HEADS * hd
    grid = (bsz // nseq, nc)
    rowi = lambda b, c: rb0 + b * nc + c
    full = lambda shape: pl.BlockSpec(shape, lambda b, c: (0,) * len(shape))

    in_specs = [
        pl.BlockSpec((r, GDN_CONV_DIM), lambda b, c: (rowi(b, c), COL_QKV // GDN_CONV_DIM)),
        pl.BlockSpec((r, GDN_WIDTH), lambda b, c: (rowi(b, c), COL_Z_GDN // GDN_WIDTH)),
        pl.BlockSpec((r, 128), lambda b, c: (rowi(b, c), COL_SMALL // 128)),
        full((CONV_WIDTH, GDN_CONV_DIM)), full((1, GDN_V_HEADS)), full((1, GDN_V_HEADS)), full((1, hd)),
        pl.BlockSpec((None, nseq, CONV_WIDTH - 1, GDN_CONV_DIM), lambda b, c: (layer, b, 0, 0)),
        pl.BlockSpec((None, nseq, sw, hd), lambda b, c: (layer, b, 0, 0)),
    ]
    args = [proj, proj, proj, lw["w_conv_gdn"], lw["gdn_dt_bias"].reshape(1, GDN_V_HEADS),
            lw["gdn_a_log"].reshape(1, GDN_V_HEADS), lw["gdn_norm"].reshape(1, hd), conv_state, gdn_state]
    st_shape, st_spec, stack, aliases = _state_out(stacked, layer, bsz, nseq, sw, hd, in_specs, args)
    out_shape = [
        jax.ShapeDtypeStruct((bsz * seq, GDN_WIDTH), BF16),
        jax.ShapeDtypeStruct((bsz, CONV_WIDTH - 1, GDN_CONV_DIM), F32),
        st_shape,
    ]
    out_specs = [
        pl.BlockSpec((r, GDN_WIDTH), lambda b, c: (b * nc + c, 0)),
        pl.BlockSpec((nseq, CONV_WIDTH - 1, GDN_CONV_DIM), lambda b, c: (b, 0, 0)),
        st_spec,
    ]
    scratch = [pltpu.VMEM((nseq, q + 8, GDN_CONV_DIM), F32), pltpu.VMEM((nseq, sw, hd), F32)]
    kern = functools.partial(_gdn_kernel, q=q, nseq=nseq, nc=nc, stack=stack)
    return pl.pallas_call(
        kern, grid=grid, in_specs=in_specs, out_specs=out_specs, out_shape=out_shape,
        scratch_shapes=scratch, input_output_aliases=aliases,
        compiler_params=_cparams(("parallel", "arbitrary")), name="gdn_mixer",
    )(*args)


def _ffn_kernel(te_ref, xi_ref, nv_ref, x_ref, wg_ref, wu_ref, wd_ref, o_ref, *scratch):
    i = pl.program_id(0)
    j = pl.program_id(1)
    valid = i < nv_ref[0]

    @pl.when(j == 0)
    def _():
        o_ref[...] = jnp.zeros(o_ref.shape, o_ref.dtype)
        if scratch:
            scratch[0][...] = x_ref[...].astype(BF16)

    @pl.when(valid)
    def _():
        x = scratch[0][...] if scratch else x_ref[...]
        gate = jnp.dot(x, wg_ref[0].astype(BF16), preferred_element_type=F32)
        up = jnp.dot(x, wu_ref[0].astype(BF16), preferred_element_type=F32)
        act = (_silu(gate) * up).astype(BF16)
        o_ref[...] += jnp.dot(act, wd_ref[0].astype(BF16), preferred_element_type=F32)


def _ffn_call(x, w_gate, w_up, w_down, tile_expert, tile_src, n_valid, tm):
    tp, d = x.shape
    f = w_gate.shape[2]
    tf = FFN_TILE_F
    nf = f // tf
    nt = tp // tm

    def jeff(i, j, nv):
        return jnp.where(i < nv[0], j, nf - 1)

    grid_spec = pltpu.PrefetchScalarGridSpec(
        num_scalar_prefetch=3,
        grid=(nt, nf),
        in_specs=[
            pl.BlockSpec((tm, d), lambda i, j, te, xi, nv: (xi[i], 0)),
            pl.BlockSpec((1, d, tf), lambda i, j, te, xi, nv: (te[i], 0, jeff(i, j, nv))),
            pl.BlockSpec((1, d, tf), lambda i, j, te, xi, nv: (te[i], 0, jeff(i, j, nv))),
            pl.BlockSpec((1, tf, d), lambda i, j, te, xi, nv: (te[i], jeff(i, j, nv), 0)),
        ],
        out_specs=pl.BlockSpec((tm, d), lambda i, j, te, xi, nv: (i, 0)),
        scratch_shapes=[] if x.dtype == BF16 else [pltpu.VMEM((tm, d), BF16)],
    )
    return pl.pallas_call(
        _ffn_kernel, grid_spec=grid_spec, out_shape=jax.ShapeDtypeStruct((tp, d), F32),
        compiler_params=_cparams(("arbitrary", "arbitrary")), name="grouped_swiglu",
    )(tile_expert, tile_src, n_valid, x, w_gate, w_up, w_down)


def _dispatch_kernel(nrows_ref, idx_now, idx_next, table, o_ref, buf, sem, *, tm):
    i = pl.program_id(0)
    nrows = nrows_ref[0]
    slot = lax.rem(i, 2)
    valid = i * tm < nrows

    @pl.when((i == 0) & valid)
    def _():
        _issue_row_gathers(table, [idx_now], buf, sem, 0, tm)

    @pl.when((i + 1) * tm < nrows)
    def _():
        _issue_row_gathers(table, [idx_next], buf, sem, 1 - slot, tm)

    @pl.when(valid)
    def _():
        _wait_row_gathers(table, buf, sem, slot, 0, tm)
        o_ref[...] = buf[slot, 0].astype(o_ref.dtype)

    @pl.when(jnp.logical_not(valid))
    def _():
        o_ref[...] = jnp.zeros(o_ref.shape, o_ref.dtype)


def _dispatch_call(table, idx, out_dtype, n_rows):
    t_out = idx.shape[0]
    d = table.shape[1]
    tm = GATHER_TILE
    nt = t_out // tm
    idx3 = idx.reshape(nt, 1, tm)
    smem = functools.partial(pl.BlockSpec, memory_space=pltpu.SMEM)
    grid_spec = pltpu.PrefetchScalarGridSpec(
        num_scalar_prefetch=1, grid=(nt,),
        in_specs=[smem((1, 1, tm), lambda i, nr: (i, 0, 0)),
                  smem((1, 1, tm), lambda i, nr: (jnp.minimum(i + 1, nt - 1), 0, 0)),
                  pl.BlockSpec(memory_space=pl.ANY)],
        out_specs=pl.BlockSpec((tm, d), lambda i, nr: (i, 0)),
        scratch_shapes=[pltpu.VMEM((2, 1, tm, d), table.dtype), pltpu.SemaphoreType.DMA((2, 1))],
    )
    return pl.pallas_call(
        functools.partial(_dispatch_kernel, tm=tm),
        grid_spec=grid_spec, out_shape=jax.ShapeDtypeStruct((t_out, d), out_dtype),
        compiler_params=_cparams(("arbitrary",)), name="row_gather",
    )(jnp.asarray(n_rows, jnp.int32).reshape(1), idx3, idx3, table)


def _moe(h, sel, cw, w_gate, w_up, w_down):
    t, d = h.shape
    ne = sel.shape[0]
    tm = MOE_TILE_M
    hot = (sel > 0).astype(jnp.int32)
    rank = jnp.cumsum(hot, axis=1) - hot
    counts = jnp.sum(hot, axis=1)
    tiles_e = (counts + tm - 1) // tm
    tile_end = jnp.cumsum(tiles_e)
    tile_start = tile_end - tiles_e
    n_valid = tile_end[-1]
    n_tiles = (2 * t + tm - 1) // tm + ne
    tp = n_tiles * tm
    dest = tile_start[:, None] * tm + rank
    tid = jnp.arange(n_tiles, dtype=jnp.int32)
    te = jnp.sum((tid[:, None] >= tile_end[None, :]).astype(jnp.int32), axis=1)
    last = jnp.maximum(n_valid - 1, 0)
    te = jnp.where(tid < n_valid, te, te[last]).astype(jnp.int32)
    xi = jnp.minimum(tid, last).astype(jnp.int32)
    p1 = jnp.sum(jnp.where(sel == 1, dest, 0), axis=0)
    p2 = jnp.sum(jnp.where(sel == 2, dest, 0), axis=0)
    wts = jnp.stack([jnp.sum(jnp.where(sel == 1, cw, 0.0), axis=0),
                     jnp.sum(jnp.where(sel == 2, cw, 0.0), axis=0)], axis=1)

    pidx = jnp.stack([p1, p2]).astype(jnp.int32)
    tok = jnp.arange(t, dtype=jnp.int32)
    src = jnp.zeros((tp,), jnp.int32).at[pidx.reshape(-1)].set(jnp.concatenate([tok, tok]), unique_indices=True)
    xs = _dispatch_call(h, src, BF16, n_valid * tm)
    ys = _ffn_call(xs, w_gate, w_up, w_down, te, xi, n_valid.reshape(1).astype(jnp.int32), tm)
    return ys, pidx, wts


_SRC_Z_SSD = 0
_SRC_X_SSD = SSD_WIDTH
_SRC_BC = 2 * SSD_WIDTH
_SRC_DT = SSD_WIDTH + SSD_CONV_DIM
_SRC_QKV = _SRC_DT + SSD_HEADS
_SRC_Z_GDN = _SRC_QKV + GDN_CONV_DIM
_SRC_A = _SRC_Z_GDN + GDN_WIDTH
_SRC_B = _SRC_A + GDN_V_HEADS
_SEGMENTS = ((COL_QKV, _SRC_QKV, GDN_CONV_DIM), (COL_Z_SSD, _SRC_Z_SSD, SSD_WIDTH),
             (COL_Z_GDN, _SRC_Z_GDN, GDN_WIDTH), (COL_X_SSD, _SRC_X_SSD, SSD_WIDTH),
             (COL_BC, _SRC_BC, SSD_CONV_DIM - SSD_WIDTH), (COL_SMALL + SMALL_DT, _SRC_DT, SSD_HEADS),
             (COL_SMALL + SMALL_A, _SRC_A, 2 * GDN_V_HEADS))
_USED_COLS = COL_SMALL + SMALL_B + GDN_V_HEADS


def _reorder_kernel(w_ref, o_ref):
    for dst, src, rows in _SEGMENTS:
        o_ref[0, dst:dst + rows, :] = w_ref[0, src:src + rows, :].astype(BF16)
    o_ref[0, _USED_COLS:PROJ_COLS, :] = jnp.zeros((PROJ_COLS - _USED_COLS, o_ref.shape[-1]), BF16)


def _reorder_in_proj(w_in):
    depth, d, n_in = w_in.shape
    tk = 512
    return pl.pallas_call(
        _reorder_kernel, grid=(depth, d // tk),
        in_specs=[pl.BlockSpec((1, n_in, tk), lambda l, k: (l, 0, k))],
        out_specs=pl.BlockSpec((1, PROJ_COLS, tk), lambda l, k: (l, 0, k)),
        out_shape=jax.ShapeDtypeStruct((depth, PROJ_COLS, d), BF16),
        compiler_params=_cparams(("parallel", "parallel")), name="reorder_w_in",
    )(jnp.swapaxes(w_in, 1, 2))


def kernel(x_prompt, x_sample, c_prompt, c_sample, state_ssd_conv, state_ssm, state_gdn_conv, state_gdn, w_ada, b_ada, w_norm_mix, w_norm_ffn, w_in, w_conv_ssd, b_conv_ssd, ssd_dt_bias, ssd_a_log, ssd_d, ssd_norm, w_conv_gdn, gdn_dt_bias, gdn_a_log, gdn_norm, w_out, w_ffn_gate, w_ffn_up, w_ffn_down, w_router, w_exp_gate, w_exp_up, w_exp_down, w_norm_final):
    bp, lp, d = x_prompt.shape
    bs, ls, _ = x_sample.shape
    depth = w_in.shape[0]
    tpr = bp * lp
    tsa = bs * ls
    t = tpr + tsa

    x = (x_prompt.reshape(tpr, d), x_sample.reshape(tsa, d))
    mod = _ada_call(jnp.concatenate([c_sample, c_prompt], axis=0), w_ada, b_ada)
    w_in_t = _reorder_in_proj(w_in)
    w_out_b = w_out.astype(BF16)

    qp = PROMPT_CHUNK if lp % PROMPT_CHUNK == 0 else lp
    nseq_s = SAMPLE_ROWS_PER_STEP // ls if ls < SAMPLE_ROWS_PER_STEP else 1
    sw = GDN_V_HEADS * GDN_HEAD
    zeros_p = (jnp.zeros((1, bp, CONV_WIDTH - 1, SSD_CONV_DIM), F32), jnp.zeros((1, bp, SSD_WIDTH, SSD_STATE), F32),
               jnp.zeros((1, bp, CONV_WIDTH - 1, GDN_CONV_DIM), F32), jnp.zeros((1, bp, sw, GDN_HEAD), F32))
    ssm_in = state_ssm.reshape(depth, bs, SSD_WIDTH, SSD_STATE)
    gdn_in = state_gdn.reshape(depth, bs, sw, GDN_HEAD)

    new_p = [[], [], [], []]
    new_s_conv = [[], []]
    ssm_out = gdn_out = None
    common = dict(t_prompt=tpr, l_prompt=lp)
    mods = [(mod[l, bs:].reshape(bp, 1, N_MOD * d), jnp.repeat(mod[l, :bs], ls, axis=0)) for l in range(depth)]
    h = _ew_call(x, w_norm_mix[0], mod=(1, 0) + mods[0], **common)
    for l in range(depth):
        mod_p, mod_s = mods[l]
        lw = {"w_conv_ssd": w_conv_ssd[l], "b_conv_ssd": b_conv_ssd[l], "ssd_dt_bias": ssd_dt_bias[l],
              "ssd_a_log": ssd_a_log[l], "ssd_d": ssd_d[l], "ssd_norm": ssd_norm[l],
              "w_conv_gdn": w_conv_gdn[l], "gdn_dt_bias": gdn_dt_bias[l], "gdn_a_log": gdn_a_log[l],
              "gdn_norm": gdn_norm[l]}
        proj = _in_proj_call(h, w_in_t, l)

        prompt = dict(row0=0, bsz=bp, seq=lp, q=qp, nseq=1)
        sample = dict(row0=tpr, bsz=bs, seq=ls, q=ls, nseq=nseq_s)
        yp_ssd, pc, pst = _ssd_call(proj, lw, zeros_p[0], zeros_p[1], 0, **prompt)
        ys_ssd, sc_, ssm_out = _ssd_call(proj, lw, state_ssd_conv, ssm_in, l, stacked=(depth, ssm_out), **sample)
        yp_gdn, pgc, pgst = _gdn_call(proj, lw, zeros_p[2], zeros_p[3], 0, **prompt)
        ys_gdn, sgc, gdn_out = _gdn_call(proj, lw, state_gdn_conv, gdn_in, l, stacked=(depth, gdn_out), **sample)
        for lst, v in zip(new_p, (pc, pst, pgc, pgst)):
            lst.append(v)
        new_s_conv[0].append(sc_)
        new_s_conv[1].append(sgc)

        mix = _out_proj_call(yp_ssd, yp_gdn, ys_ssd, ys_gdn, w_out_b[l])
        moe_layer = l % 2 == 1
        i = l // 2
        if moe_layer:
            x, h2, sel, cw = _ew_call(x, w_norm_ffn[l], res=(mix, 2, mod_p, mod_s), mod=(4, 3, mod_p, mod_s),
                                      out_x=True, h_dtype=F32, router_w=w_router[i], **common)
            f = _moe(h2, sel, cw, w_exp_gate[i], w_exp_up[i], w_exp_down[i])
        else:
            x, h2 = _ew_call(x, w_norm_ffn[l], res=(mix, 2, mod_p, mod_s), mod=(4, 3, mod_p, mod_s), out_x=True,
                             **common)
            nt = t // FFN_TILE_M
            tid = jnp.arange(nt, dtype=jnp.int32)
            f = _ffn_call(h2, w_ffn_gate[i][None], w_ffn_up[i][None], w_ffn_down[i][None],
                          jnp.zeros((nt,), jnp.int32), tid, jnp.full((1,), nt, jnp.int32), FFN_TILE_M)
        if l + 1 < depth:
            x, h = _ew_call(x, w_norm_mix[l + 1], res=(f, 5, mod_p, mod_s), mod=(1, 0) + mods[l + 1],
                            out_x=True, **common)
        else:
            y_p, y_s = _ew_call(x, w_norm_final, res=(f, 5, mod_p, mod_s), h_dtype=F32, split_out=True, **common)

    ssm_shape = (SSD_HEADS, SSD_HEAD_DIM, SSD_STATE)
    gdn_shape = (GDN_V_HEADS, GDN_HEAD, GDN_HEAD)
    prompt_states = (jnp.stack(new_p[0]), jnp.stack(new_p[1]).reshape((depth, bp) + ssm_shape),
                     jnp.stack(new_p[2]), jnp.stack(new_p[3]).reshape((depth, bp) + gdn_shape))
    sample_states = (jnp.stack(new_s_conv[0]), ssm_out.reshape((depth, bs) + ssm_shape),
                     jnp.stack(new_s_conv[1]), gdn_out.reshape((depth, bs) + gdn_shape))
    return (y_p.reshape(bp, lp, d), y_s.reshape(bs, ls, d)) + prompt_states + sample_states
```
